```python
import math
import jax, jax.numpy as jnp
from jax import lax
import numpy as np

D_MODEL = 2048
BATCH = 16
SEQ = 256
DEPTH = 2
DEC_BATCH = 8
DEC_SEQ = 1024
PAST_LEN = 256

GRID_W = 64
N_EVEN = (DEPTH + 1) // 2
N_ODD = DEPTH // 2
HEAD_QK = 64
HEAD_V = 2 * HEAD_QK
H_A = D_MODEL // (2 * HEAD_V)
D_A_Q = H_A * 2 * HEAD_QK
D_A_V = H_A * HEAD_V
HEAD_B = 64
H_B = D_MODEL // (2 * HEAD_B)
D_B = H_B * HEAD_B
LORA_W = 96
LORA_A = 96
LORA_G = 256
D_SHIFT = 3 * D_B + LORA_W + LORA_A + LORA_G
D_IN = 2 * D_A_Q + D_A_V + D_SHIFT
D_MIX_OUT = D_A_V + D_B
D_CONV = D_MODEL
CONV_W = 31
D_FF = 5504
FFN_CONV_W = 3

RMS_EPS = 1e-6
LN_EPS = 1e-5
GN_EPS = 64e-5
Q_BLOCK = 128
ROPE_BASE = 10000.0
F32 = jnp.float32

kernel_name = 'hybrid_diffattn_rwkv7_conformer_dit_step'


def rmsnorm(x, g, eps=RMS_EPS):
    xf = x.astype(F32)
    y = xf * lax.rsqrt(jnp.mean(xf * xf, axis=-1, keepdims=True) + eps)
    return (y * g.astype(F32)).astype(x.dtype)


def layernorm(x, g, b, eps=LN_EPS):
    xf = x.astype(F32)
    mu = jnp.mean(xf, axis=-1, keepdims=True)
    var = jnp.mean(jnp.square(xf - mu), axis=-1, keepdims=True)
    y = (xf - mu) * lax.rsqrt(var + eps)
    return (y * g.astype(F32) + b.astype(F32)).astype(x.dtype)


def dwconv(x, w):
    k, ch = w.shape
    pad = (k - 1) // 2
    return lax.conv_general_dilated(x, w[:, None, :].astype(x.dtype), window_strides=(1,),
                                    padding=((pad, pad),), dimension_numbers=('NWC', 'WIO', 'NWC'),
                                    feature_group_count=ch)


def adaln(cvec, w, b):
    return (jax.nn.silu(cvec) @ w + b)[:, None, :]


def rope_2d(x):
    t = x.shape[-2]
    rows = t // GRID_W
    row = jnp.repeat(jnp.arange(rows), GRID_W).astype(F32)
    col = jnp.tile(jnp.arange(GRID_W), rows).astype(F32)
    half = HEAD_QK // 2
    n_freq = half // 2
    inv = ROPE_BASE ** (-jnp.arange(n_freq, dtype=F32) / n_freq)
    xf = x.astype(F32)

    def rot(xs, pos):
        ang = pos[:, None] * inv[None, :]
        cos, sin = jnp.cos(ang), jnp.sin(ang)
        x1, x2 = xs[..., :n_freq], xs[..., n_freq:]
        return jnp.concatenate([x1 * cos - x2 * sin, x1 * sin + x2 * cos], axis=-1)

    out = jnp.concatenate([rot(xf[..., :half], row), rot(xf[..., half:], col)], axis=-1)
    return out.astype(x.dtype)


def diff_attention(q, k, v, lam):
    b, h, _, tq, d = q.shape
    dv = v.shape[-1]
    nblk = tq // Q_BLOCK
    qb = q.astype(F32).reshape(b, h, 2, nblk, Q_BLOCK, d).transpose(3, 0, 1, 2, 4, 5)
    kf = k.astype(F32) * (d ** -0.5)
    vf = v.astype(F32)

    def one_block(q_blk):
        s = jnp.einsum('bhmqd,bhmkd->bhmqk', q_blk, kf)
        p = jax.nn.softmax(s, axis=-1)
        a = p[:, :, 0] - lam * p[:, :, 1]
        return jnp.einsum('bhqk,bhkv->bhqv', a, vf)

    o = lax.map(one_block, qb)
    return o.transpose(1, 2, 0, 3, 4).reshape(b, h, tq, dv)


def wkv7_scan(r, w, k, v, a, b, s0, reverse):
    xs = tuple(jnp.moveaxis(t, 1, 0) for t in (r, w, k, v, a, b))

    def step(s, inp):
        r_t, w_t, k_t, v_t, a_t, b_t = inp
        sa = jnp.einsum('bhvk,bhk->bhv', s, a_t)
        s = s * w_t[:, :, None, :] + sa[..., None] * b_t[:, :, None, :] + v_t[..., None] * k_t[:, :, None, :]
        y = jnp.einsum('bhvk,bhk->bhv', s, r_t)
        return s, y

    s_fin, ys = lax.scan(step, s0.astype(F32), xs, reverse=reverse)
    return jnp.moveaxis(ys, 0, 1), s_fin


def ab_mixer(h, w_in, w_out, diff_lambda, subln_g, shift_mu, w0, w2, a0, a2, g2, k_k, k_a, r_k,
             lnx_g, lnx_b, lam_init, ctx):
    bsz, t, _ = h.shape
    proj = h @ w_in
    qa, ka, va, rest = jnp.split(proj, [D_A_Q, 2 * D_A_Q, 2 * D_A_Q + D_A_V], axis=-1)
    q = qa.reshape(bsz, t, H_A, 2, HEAD_QK).transpose(0, 2, 3, 1, 4)
    k = ka.reshape(bsz, t, H_A, 2, HEAD_QK).transpose(0, 2, 3, 1, 4)
    v = va.reshape(bsz, t, H_A, HEAD_V).transpose(0, 2, 1, 3)
    if ctx is None:
        k_all, v_all = k, v
        s0_f = jnp.zeros((bsz, H_B, HEAD_B, HEAD_B), F32)
        s0_b = jnp.zeros((bsz, H_B, HEAD_B, HEAD_B), F32)
    else:
        k_ctx, v_ctx, s0_f, s0_b = ctx
        q, k = rope_2d(q), rope_2d(k)
        k_all = jnp.concatenate([k_ctx.astype(k.dtype), k], axis=3)
        v_all = jnp.concatenate([v_ctx.astype(v.dtype), v], axis=2)
    dl = diff_lambda.astype(F32)
    lam = jnp.exp(jnp.sum(dl[0] * dl[1])) - jnp.exp(jnp.sum(dl[2] * dl[3])) + lam_init
    o = diff_attention(q, k_all, v_all, lam)
    o = rmsnorm(o, subln_g, LN_EPS) * (1.0 - lam_init)
    o_a = o.transpose(0, 2, 1, 3).reshape(bsz, t, D_A_V).astype(h.dtype)
    p = rest.astype(F32)
    p_prev = jnp.pad(p, ((0, 0), (1, 0), (0, 0)))[:, :t]
    p_next = jnp.pad(p, ((0, 0), (0, 1), (0, 0)))[:, 1:]
    mu = shift_mu.astype(F32)
    p = p + mu[0] * (p_prev - p) + mu[1] * (p_next - p)
    r, kb, vb, xw, xa, xg = jnp.split(
        p, [D_B, 2 * D_B, 3 * D_B, 3 * D_B + LORA_W, 3 * D_B + LORA_W + LORA_A], axis=-1)
    heads = lambda z: z.reshape(bsz, t, H_B, HEAD_B)
    g = jax.nn.sigmoid(xg) @ g2.astype(F32)
    kk = heads(kb * k_k.astype(F32))
    kk = kk / jnp.maximum(jnp.sqrt(jnp.sum(kk * kk, axis=-1, keepdims=True)), 1e-12)
    y = jnp.zeros((bsz, t, H_B, HEAD_B), F32)
    finals = []
    for d, s0 in enumerate((s0_f, s0_b)):
        w_log = -jax.nn.softplus(-(w0[d].astype(F32) + jnp.tanh(xw) @ w2[d].astype(F32))) - 0.5
        decay = jnp.exp(-jnp.exp(w_log))
        a_lr = jax.nn.sigmoid(a0[d].astype(F32) + xa @ a2[d].astype(F32))
        k_d = kb * (1.0 + (a_lr - 1.0) * k_a.astype(F32))
        y_d, s_d = wkv7_scan(heads(r), heads(decay), heads(k_d), heads(vb), -kk,
                             kk * heads(a_lr), s0, reverse=(d == 1))
        y = y + y_d
        finals.append(s_d.astype(h.dtype))
    mu_y = jnp.mean(y, axis=-1, keepdims=True)
    var_y = jnp.mean(jnp.square(y - mu_y), axis=-1, keepdims=True)
    yn = ((y - mu_y) * lax.rsqrt(var_y + GN_EPS)).reshape(bsz, t, D_B)
    yn = yn * lnx_g.astype(F32) + lnx_b.astype(F32)
    bonus = jnp.sum(heads(r) * heads(kb) * r_k.astype(F32), axis=-1, keepdims=True) * heads(vb)
    o_b = ((yn + bonus.reshape(bsz, t, D_B)) * g).astype(h.dtype)
    out = jnp.concatenate([o_a, o_b], axis=-1) @ w_out
    return out, k, v, finals[0], finals[1]


def conformer_conv(h, w_pw1, w_dw, b_dw, cln_g, cln_b, w_pw2):
    u = h @ w_pw1
    u = u[..., :D_CONV] * jax.nn.sigmoid(u[..., D_CONV:])
    u = dwconv(u, w_dw) + b_dw
    u = jax.nn.silu(layernorm(u, cln_g, cln_b))
    return u @ w_pw2


def conv_ffn(h, w_up, w_dw, w_down):
    u = dwconv(h @ w_up, w_dw)
    gate, val = jnp.split(u, 2, axis=-1)
    return (jax.nn.silu(gate) * val) @ w_down


def setup_inputs(seed: int = 0) -> dict:
    key = jax.random.key(seed)
    ks = iter(jax.random.split(key, 48))
    nrm = lambda shape, s: jax.random.normal(next(ks), shape, F32) * s
    return {
        'x_prompt': nrm((BATCH, SEQ, D_MODEL), 1.0),
        'x_sample': nrm((DEC_BATCH, DEC_SEQ, D_MODEL), 1.0),
        'cache_k': nrm((DEC_BATCH, N_EVEN, H_A, 2, PAST_LEN, HEAD_QK), 1.0),
        'cache_v': nrm((DEC_BATCH, N_EVEN, H_A, PAST_LEN, HEAD_V), 1.0),
        'state_wkv_fwd': nrm((DEC_BATCH, N_EVEN, H_B, HEAD_B, HEAD_B), 0.3),
        'state_wkv_bwd': nrm((DEC_BATCH, N_EVEN, H_B, HEAD_B, HEAD_B), 0.3),
        'c': nrm((DEC_BATCH, D_MODEL), 1.0),
        'c_ctx': nrm((D_MODEL,), 1.0),
        'w_ada': nrm((DEPTH, D_MODEL, 6 * D_MODEL), 0.5 * D_MODEL ** -0.5),
        'b_ada': nrm((DEPTH, 6 * D_MODEL), 0.01),
        'norm_g': 1.0 + nrm((DEPTH, 2, D_MODEL), 0.02),
        'final_norm_g': 1.0 + nrm((D_MODEL,), 0.02),
        'w_in': nrm((N_EVEN, D_MODEL, D_IN), D_MODEL ** -0.5),
        'w_out': nrm((N_EVEN, D_MIX_OUT, D_MODEL), D_MIX_OUT ** -0.5),
        'diff_lambda': nrm((N_EVEN, 4, HEAD_QK), 0.1),
        'subln_g': 1.0 + nrm((N_EVEN, HEAD_V), 0.02),
        'shift_mu': 0.3 + nrm((N_EVEN, 2, D_SHIFT), 0.1),
        'w0': -2.0 + nrm((N_EVEN, 2, D_B), 0.5),
        'w2': nrm((N_EVEN, 2, LORA_W, D_B), 0.5 * LORA_W ** -0.5),
        'a0': nrm((N_EVEN, 2, D_B), 0.1),
        'a2': nrm((N_EVEN, 2, LORA_A, D_B), 0.5 * LORA_A ** -0.5),
        'g2': nrm((N_EVEN, LORA_G, D_B), LORA_G ** -0.5),
        'k_k': 0.85 + nrm((N_EVEN, D_B), 0.05),
        'k_a': 1.0 + nrm((N_EVEN, D_B), 0.05),
        'r_k': nrm((N_EVEN, H_B, HEAD_B), 0.1),
        'lnx_g': 1.0 + nrm((N_EVEN, D_B), 0.02),
        'lnx_b': nrm((N_EVEN, D_B), 0.01),
        'w_pw1': nrm((N_ODD, D_MODEL, 2 * D_CONV), D_MODEL ** -0.5),
        'w_dw': nrm((N_ODD, CONV_W, D_CONV), CONV_W ** -0.5),
        'b_dw': nrm((N_ODD, D_CONV), 0.01),
        'cln_g': 1.0 + nrm((N_ODD, D_CONV), 0.02),
        'cln_b': nrm((N_ODD, D_CONV), 0.01),
        'w_pw2': nrm((N_ODD, D_CONV, D_MODEL), D_CONV ** -0.5),
        'w_up': nrm((DEPTH, D_MODEL, 2 * D_FF), D_MODEL ** -0.5),
        'w_ffn_dw': nrm((DEPTH, FFN_CONV_W, 2 * D_FF), FFN_CONV_W ** -0.5),
        'w_down': nrm((DEPTH, D_FF, D_MODEL), D_FF ** -0.5),
    }


def reference(x_prompt, x_sample, cache_k, cache_v, state_wkv_fwd, state_wkv_bwd, c, c_ctx,
              w_ada, b_ada, norm_g, final_norm_g, w_in, w_out, diff_lambda, subln_g, shift_mu,
              w0, w2, a0, a2, g2, k_k, k_a, r_k, lnx_g, lnx_b, w_pw1, w_dw, b_dw, cln_g, cln_b,
              w_pw2, w_up, w_ffn_dw, w_down):
    xp, xs = x_prompt, x_sample
    new_k, new_v, new_sf, new_sb = [], [], [], []
    for l in range(DEPTH):
        sh1p, sc1p, g1p, sh2p, sc2p, g2p = jnp.split(adaln(c_ctx[None, :], w_ada[l], b_ada[l]), 6, axis=-1)
        sh1s, sc1s, g1s, sh2s, sc2s, g2s = jnp.split(adaln(c, w_ada[l], b_ada[l]), 6, axis=-1)
        hp = rmsnorm(xp, norm_g[l, 0]) * (1.0 + sc1p) + sh1p
        hs = rmsnorm(xs, norm_g[l, 0]) * (1.0 + sc1s) + sh1s
        if l % 2 == 0:
            e = l // 2
            prm = (w_in[e], w_out[e], diff_lambda[e], subln_g[e], shift_mu[e], w0[e], w2[e], a0[e],
                   a2[e], g2[e], k_k[e], k_a[e], r_k[e], lnx_g[e], lnx_b[e])
            lam_init = 0.8 - 0.6 * math.exp(-0.3 * l)
            mp, kc, vc, sf, sb = ab_mixer(hp, *prm, lam_init=lam_init, ctx=None)
            ms, _, _, _, _ = ab_mixer(hs, *prm, lam_init=lam_init,
                                      ctx=(cache_k[:, e], cache_v[:, e], state_wkv_fwd[:, e], state_wkv_bwd[:, e]))
            new_k.append(kc)
            new_v.append(vc)
            new_sf.append(sf)
            new_sb.append(sb)
        else:
            o = l // 2
            prm = (w_pw1[o], w_dw[o], b_dw[o], cln_g[o], cln_b[o], w_pw2[o])
            mp = conformer_conv(hp, *prm)
            ms = conformer_conv(hs, *prm)
        xp = xp + g1p * mp
        xs = xs + g1s * ms
        hp = rmsnorm(xp, norm_g[l, 1]) * (1.0 + sc2p) + sh2p
        hs = rmsnorm(xs, norm_g[l, 1]) * (1.0 + sc2s) + sh2s
        xp = xp + g2p * conv_ffn(hp, w_up[l], w_ffn_dw[l], w_down[l])
        xs = xs + g2s * conv_ffn(hs, w_up[l], w_ffn_dw[l], w_down[l])
    y_prompt = rmsnorm(xp, final_norm_g)
    y_sample = rmsnorm(xs, final_norm_g)
    new_cache_k = jnp.stack(new_k, axis=1)
    new_cache_v = jnp.stack(new_v, axis=1)
    new_state_fwd = jnp.stack(new_sf, axis=1)
    new_state_bwd = jnp.stack(new_sb, axis=1)
    return (y_prompt, y_sample, new_cache_k, new_cache_v, new_state_fwd, new_state_bwd)
```

```python
import functools
import math

import jax
import jax.numpy as jnp
from jax import lax
from jax.experimental import pallas as pl
from jax.experimental.pallas import tpu as pltpu

F32 = jnp.float32
BF16 = jnp.bfloat16

LANES = 128
SUBLANES = 8
VMEM_LIMIT_BYTES = 56 * 1024 * 1024

D_MODEL = 2048
BLK = 256
GRID_W = 64
HEAD_QK = 64
HEAD_V = 128
H_A = 8
D_A = 1024
HEAD_B = 64
H_B = 16
D_B = 1024
N_PAIR = D_B // LANES
LORA_W = 96
LORA_A = 96
LORA_G = 256
D_REST = 3 * D_B + 2 * LANES + LORA_G
D_PROJ = D_REST + 3 * D_A
D_FF = 5504
D_FF_PAD = 5632
CONV_W = 31
CONV_PAD = (CONV_W - 1) // 2
CONV_HALO = 16
RMS_EPS = 1e-6
LN_EPS = 1e-5
GN_EPS = 64e-5
ROPE_BASE = 10000.0
LAM_INIT0 = 0.8 - 0.6 * math.exp(-0.3 * 0)
N_SPLIT = 3
N_COND = 16


def _cparams(*sem):
    return pltpu.CompilerParams(dimension_semantics=sem, vmem_limit_bytes=VMEM_LIMIT_BYTES)


class _Layout:
    def __init__(self, n_prompt, n_sample, sample_len):
        self.n_prompt = n_prompt
        self.n_sample = n_sample
        self.bps = sample_len // BLK
        self.pb = n_prompt
        self.nb = n_prompt + n_sample * self.bps
        self.rows = self.nb * BLK
        assert 1 + n_sample <= N_COND

    def cond_of_block(self, i):
        return jnp.where(i < self.pb, 0, 1 + (i - self.pb) // self.bps)

    def seq_pos(self, i):
        c = (i - self.pb) % self.bps
        first = jnp.logical_or(i < self.pb, c == 0)
        last = jnp.logical_or(i < self.pb, c == self.bps - 1)
        return first, last

    def tile_rows(self):
        t = self.bps * BLK
        while (self.pb * BLK) % t:
            t //= 2
        return t


def _adaln_kernel(c_ref, w_ref, b_ref, o_ref):
    x = c_ref[...]
    s = (x * jax.nn.sigmoid(x)).astype(BF16)
    o_ref[0] = jnp.dot(s, w_ref[0].astype(BF16), preferred_element_type=F32) + b_ref[0]


def _adaln(c16, w_ada, b_ada):
    depth, d, n = w_ada.shape
    tn = 1024
    return pl.pallas_call(
        _adaln_kernel,
        grid=(depth, n // tn),
        in_specs=[pl.BlockSpec((N_COND, d), lambda l, j: (0, 0)),
                  pl.BlockSpec((1, d, tn), lambda l, j: (l, 0, j)),
                  pl.BlockSpec((1, 1, tn), lambda l, j: (l, 0, j))],
        out_specs=pl.BlockSpec((1, N_COND, tn), lambda l, j: (l, 0, j)),
        out_shape=jax.ShapeDtypeStruct((depth, N_COND, n), F32),
        compiler_params=_cparams("parallel", "parallel"),
        name="adaln",
    )(c16, w_ada, b_ada.reshape(depth, 1, n))


def _norm_mod_kernel(x_ref, g_ref, sh_ref, sc_ref, o_ref):
    x = x_ref[...]
    y = x * lax.rsqrt(jnp.mean(x * x, axis=-1, keepdims=True) + RMS_EPS) * g_ref[...]
    o_ref[...] = (y * (1.0 + sc_ref[0]) + sh_ref[0]).astype(o_ref.dtype)


def _norm_mod(lay, x, g, mod, shift_idx):
    d = x.shape[1]
    cond = lay.cond_of_block
    return pl.pallas_call(
        _norm_mod_kernel,
        grid=(lay.nb,),
        in_specs=[pl.BlockSpec((BLK, d), lambda i: (i, 0)),
                  pl.BlockSpec((1, d), lambda i: (0, 0)),
                  pl.BlockSpec((1, 1, d), lambda i: (cond(i), 0, shift_idx)),
                  pl.BlockSpec((1, 1, d), lambda i: (cond(i), 0, shift_idx + 1))],
        out_specs=pl.BlockSpec((BLK, d), lambda i: (i, 0)),
        out_shape=jax.ShapeDtypeStruct(x.shape, BF16),
        compiler_params=_cparams("parallel"),
        name="norm_mod",
    )(x, g.reshape(1, d), mod, mod)


def _final_norm_kernel(x_ref, g_ref, o_ref):
    x = x_ref[...]
    o_ref[...] = x * lax.rsqrt(jnp.mean(x * x, axis=-1, keepdims=True) + RMS_EPS) * g_ref[...]


def _final_norm(x, g, blk0, nblk):
    d = x.shape[1]
    return pl.pallas_call(
        _final_norm_kernel,
        grid=(nblk,),
        in_specs=[pl.BlockSpec((BLK, d), lambda i: (i + blk0, 0)),
                  pl.BlockSpec((1, d), lambda i: (0, 0))],
        out_specs=pl.BlockSpec((BLK, d), lambda i: (i, 0)),
        out_shape=jax.ShapeDtypeStruct((nblk * BLK, d), F32),
        compiler_params=_cparams("parallel"),
        name="final_norm",
    )(x, g.reshape(1, d))


def _mm_kernel(a_ref, w_ref, o_ref):
    o_ref[...] = jnp.dot(a_ref[...], w_ref[...], preferred_element_type=F32)


def _mm(a, w, tm, tn):
    m, k = a.shape
    n = w.shape[1]
    return pl.pallas_call(
        _mm_kernel,
        grid=(m // tm, n // tn),
        in_specs=[pl.BlockSpec((tm, k), lambda i, j: (i, 0)),
                  pl.BlockSpec((k, tn), lambda i, j: (0, j))],
        out_specs=pl.BlockSpec((tm, tn), lambda i, j: (i, j)),
        out_shape=jax.ShapeDtypeStruct((m, n), F32),
        compiler_params=_cparams("parallel", "parallel"),
        name="mm",
    )(a, w)


def _mm_res_kernel(*refs, n_in):
    a_refs, w_refs = refs[:n_in], refs[n_in:2 * n_in]
    x_ref, g_ref, o_ref = refs[2 * n_in:]
    acc = jnp.dot(a_refs[0][...], w_refs[0][...], preferred_element_type=F32)
    for a_ref, w_ref in zip(a_refs[1:], w_refs[1:]):
        acc += jnp.dot(a_ref[...], w_ref[...], preferred_element_type=F32)
    o_ref[...] = x_ref[...] + g_ref[0] * acc


def _mm_res(lay, a_list, w_list, x, mod, gate_idx, tm, tn):
    m, n = x.shape
    bpt = tm // BLK
    cond = lambda i: lay.cond_of_block(i * bpt)
    in_specs = ([pl.BlockSpec((tm, a.shape[1]), lambda i, j: (i, 0)) for a in a_list]
                + [pl.BlockSpec((w.shape[0], tn), lambda i, j: (0, j)) for w in w_list]
                + [pl.BlockSpec((tm, tn), lambda i, j: (i, j)),
                   pl.BlockSpec((1, 1, tn), lambda i, j: (cond(i), 0, gate_idx * (n // tn) + j))])
    return pl.pallas_call(
        functools.partial(_mm_res_kernel, n_in=len(a_list)),
        grid=(m // tm, n // tn),
        in_specs=in_specs,
        out_specs=pl.BlockSpec((tm, tn), lambda i, j: (i, j)),
        out_shape=jax.ShapeDtypeStruct((m, n), F32),
        compiler_params=_cparams("parallel", "parallel"),
        name="mm_res",
    )(*a_list, *w_list, x, mod)


def _mm_glu_kernel(a_ref, wa_ref, wb_ref, o_ref):
    a = a_ref[...]
    u = jnp.dot(a, wa_ref[...], preferred_element_type=F32)
    v = jnp.dot(a, wb_ref[...], preferred_element_type=F32)
    o_ref[...] = u * jax.nn.sigmoid(v)


def _mm_glu(a, w, tm, tn):
    m, k = a.shape
    n = w.shape[1] // 2
    nj = n // tn
    return pl.pallas_call(
        _mm_glu_kernel,
        grid=(m // tm, nj),
        in_specs=[pl.BlockSpec((tm, k), lambda i, j: (i, 0)),
                  pl.BlockSpec((k, tn), lambda i, j: (0, j)),
                  pl.BlockSpec((k, tn), lambda i, j: (0, j + nj))],
        out_specs=pl.BlockSpec((tm, tn), lambda i, j: (i, j)),
        out_shape=jax.ShapeDtypeStruct((m, n), F32),
        compiler_params=_cparams("parallel", "parallel"),
        name="mm_glu",
    )(a, w, w)


def _row_index(shape):
    return lax.broadcasted_iota(jnp.int32, shape, 0)


def _ffn_up_kernel(a_ref, wg_ref, wv_ref, dg_ref, dv_ref, o_ref, *, lay, tm):
    i = pl.program_id(0)
    seq_len = jnp.where(i * (tm // BLK) < lay.pb, BLK, lay.bps * BLK)
    a = a_ref[...]

    def conv(w_ref, d_ref):
        u = jnp.dot(a, w_ref[...], preferred_element_type=F32)
        pos = _row_index(u.shape) & (seq_len - 1)
        prev = jnp.where(pos == 0, 0.0, pltpu.roll(u, 1, axis=0))
        nxt = jnp.where(pos == seq_len - 1, 0.0, pltpu.roll(u, tm - 1, axis=0))
        return prev * d_ref[0:1, :] + u * d_ref[1:2, :] + nxt * d_ref[2:3, :]

    gate = conv(wg_ref, dg_ref)
    val = conv(wv_ref, dv_ref)
    o_ref[...] = (gate * jax.nn.sigmoid(gate) * val).astype(o_ref.dtype)


def _ffn_up(lay, a, w_up, w_dw, tm, tn):
    m, k = a.shape
    f = w_up.shape[1] // 2
    nj = f // tn
    return pl.pallas_call(
        functools.partial(_ffn_up_kernel, lay=lay, tm=tm),
        grid=(m // tm, nj),
        in_specs=[pl.BlockSpec((tm, k), lambda i, j: (i, 0)),
                  pl.BlockSpec((k, tn), lambda i, j: (0, j)),
                  pl.BlockSpec((k, tn), lambda i, j: (0, j + nj)),
                  pl.BlockSpec((3, tn), lambda i, j: (0, j)),
                  pl.BlockSpec((3, tn), lambda i, j: (0, j + nj))],
        out_specs=pl.BlockSpec((tm, tn), lambda i, j: (i, j)),
        out_shape=jax.ShapeDtypeStruct((m, f), BF16),
        compiler_params=_cparams("parallel", "parallel"),
        name="ffn_up",
    )(a, w_up, w_up, w_dw, w_dw)


def _split_bf16(x):
    parts = []
    r = x
    for s in range(N_SPLIT):
        p = r.astype(BF16)
        parts.append(p)
        if s + 1 < N_SPLIT:
            r = r - p.astype(F32)
    return parts


def _seg_sum(x, e_ref):
    lhs = jnp.concatenate(_split_bf16(x), axis=1)
    return jnp.dot(lhs, e_ref[...], preferred_element_type=F32)


def _seg_ones():
    r = jnp.arange(N_SPLIT * LANES)[:, None] % LANES
    c = jnp.arange(LANES)[None, :]
    return (r // HEAD_B == c // HEAD_B).astype(BF16)


def _softplus(z):
    return jnp.maximum(z, 0.0) + jnp.log(1.0 + jnp.exp(-jnp.abs(z)))


def _prep_kernel(p_ref, pp_ref, pn_ref, mu_ref, w0_ref, w2_ref, a0_ref, a2_ref, g2_ref,
                 kk_ref, ka_ref, rk_ref, e_ref,
                 r_o, v_o, a_o, wf_o, kf_o, bf_o, wb_o, kb_o, bb_o, g_o, bonus_o, *, lay):
    i = pl.program_id(0)
    first, last = lay.seq_pos(i)
    rows = _row_index((BLK, 1))

    def shifted(c0, c1):
        p = p_ref[:, c0:c1]
        prev_row = jnp.where(first, 0.0, pp_ref[SUBLANES - 1:SUBLANES, c0:c1])
        next_row = jnp.where(last, 0.0, pn_ref[0:1, c0:c1])
        p_prev = jnp.where(rows == 0, prev_row, pltpu.roll(p, 1, axis=0))
        p_next = jnp.where(rows == BLK - 1, next_row, pltpu.roll(p, BLK - 1, axis=0))
        return p + mu_ref[0:1, c0:c1] * (p_prev - p) + mu_ref[1:2, c0:c1] * (p_next - p)

    c_w = 3 * D_B
    c_a = c_w + LANES
    c_g = c_a + LANES
    tw = jnp.tanh(shifted(c_w, c_a)).astype(BF16)
    xa = shifted(c_a, c_g).astype(BF16)
    sg = jax.nn.sigmoid(shifted(c_g, D_REST)).astype(BF16)

    for j in range(N_PAIR):
        sl = slice(j * LANES, (j + 1) * LANES)
        r = shifted(j * LANES, (j + 1) * LANES)
        kb = shifted(D_B + j * LANES, D_B + (j + 1) * LANES)
        vb = shifted(2 * D_B + j * LANES, 2 * D_B + (j + 1) * LANES)
        r_o[:, sl] = r
        v_o[:, sl] = vb
        g_o[:, sl] = jnp.dot(sg, g2_ref[:, sl], preferred_element_type=F32)
        kk = kb * kk_ref[:, sl]
        nrm = jnp.sqrt(_seg_sum(kk * kk, e_ref))
        kk = kk / jnp.maximum(nrm, 1e-12)
        a_o[:, sl] = -kk
        bonus_o[:, sl] = _seg_sum(r * kb * rk_ref[:, sl], e_ref) * vb
        for d, (w_o, k_o, b_o) in enumerate(((wf_o, kf_o, bf_o), (wb_o, kb_o, bb_o))):
            lw = w0_ref[d:d + 1, sl] + jnp.dot(tw, w2_ref[d, :, sl], preferred_element_type=F32)
            w_log = -_softplus(-lw) - 0.5
            w_o[:, sl] = jnp.exp(-jnp.exp(w_log))
            a_lr = jax.nn.sigmoid(a0_ref[d:d + 1, sl]
                                  + jnp.dot(xa, a2_ref[d, :, sl], preferred_element_type=F32))
            k_o[:, sl] = kb * (1.0 + (a_lr - 1.0) * ka_ref[:, sl])
            b_o[:, sl] = kk * a_lr


def _rwkv_prep(lay, proj, mu, w0, w2p, a0, a2p, g2, k_k, k_a, r_k, e3):
    nb = lay.nb
    hb = BLK // SUBLANES
    n_halo = lay.rows // SUBLANES
    full = lambda shape: pl.BlockSpec(shape, lambda i: (0,) * len(shape))
    out = jax.ShapeDtypeStruct((lay.rows, D_B), F32)
    return pl.pallas_call(
        functools.partial(_prep_kernel, lay=lay),
        grid=(nb,),
        in_specs=[pl.BlockSpec((BLK, D_REST), lambda i: (i, 0)),
                  pl.BlockSpec((SUBLANES, D_REST), lambda i: (jnp.maximum(i * hb - 1, 0), 0)),
                  pl.BlockSpec((SUBLANES, D_REST), lambda i: (jnp.minimum((i + 1) * hb, n_halo - 1), 0)),
                  full((2, D_REST)), full((2, D_B)), full((2, LANES, D_B)), full((2, D_B)),
                  full((2, LANES, D_B)), full((LORA_G, D_B)), full((1, D_B)), full((1, D_B)),
                  full((1, D_B)), full((N_SPLIT * LANES, LANES))],
        out_specs=[pl.BlockSpec((BLK, D_B), lambda i: (i, 0))] * 11,
        out_shape=[out] * 11,
        compiler_params=_cparams("parallel"),
        name="rwkv_prep",
    )(proj, proj, proj, mu, w0, w2p, a0, a2p, g2, k_k, k_a, r_k, e3)


def _scan_kernel(rf, vf, af, wf, kf, bf, rb, vb, ab, wb, kb, bb, init_ref, e_ref,
                 yf_ref, yb_ref, fin_ref, st, ybuf, *, lay):
    i = pl.program_id(0)
    first, _ = lay.seq_pos(i)

    @pl.when(first)
    def _():
        st[...] = jnp.where(i >= lay.pb, init_ref[0], 0.0)

    diag = (lax.broadcasted_iota(jnp.int32, (HEAD_B, LANES), 0)
            == lax.broadcasted_iota(jnp.int32, (HEAD_B, LANES), 1) % HEAD_B)
    dirs = ((rf, vf, af, wf, kf, bf, yf_ref), (rb, vb, ab, wb, kb, bb, yb_ref))

    def seg_sums(xs):
        lhs = jnp.concatenate([jnp.concatenate(_split_bf16(x), axis=1) for x in xs], axis=0)
        res = jnp.dot(lhs, e_ref[...], preferred_element_type=F32)
        return [res[k * HEAD_B:(k + 1) * HEAD_B] for k in range(len(xs))]

    def group(gi, carry):
        bases = (pl.multiple_of(gi * SUBLANES, SUBLANES),
                 pl.multiple_of(BLK - SUBLANES - gi * SUBLANES, SUBLANES))

        def row(ref, d, j, sl):
            return ref[pl.ds(bases[d], SUBLANES), sl][j:j + 1, :]

        for t in range(SUBLANES):
            js = (t, SUBLANES - 1 - t)
            xs = []
            for d in range(2):
                a_r, v_r = dirs[d][2], dirs[d][1]
                for p in range(N_PAIR):
                    sl = slice(p * LANES, (p + 1) * LANES)
                    xs.append(st[d, p] * row(a_r, d, js[d], sl))
                    xs.append(jnp.where(diag, row(v_r, d, js[d], sl), 0.0))
            res1 = seg_sums(xs)
            xs = []
            for d in range(2):
                r_r, _, _, w_r, k_r, b_r, _ = dirs[d]
                for p in range(N_PAIR):
                    sl = slice(p * LANES, (p + 1) * LANES)
                    idx = d * N_PAIR + p
                    sa, vcol = res1[2 * idx], res1[2 * idx + 1]
                    s_new = (st[d, p] * row(w_r, d, js[d], sl) + sa * row(b_r, d, js[d], sl)
                             + vcol * row(k_r, d, js[d], sl))
                    st[d, p] = s_new
                    xs.append(s_new * row(r_r, d, js[d], sl))
            res2 = seg_sums(xs)
            for d in range(2):
                for p in range(N_PAIR):
                    ybuf[d, js[d]:js[d] + 1, p * LANES:(p + 1) * LANES] = jnp.sum(
                        jnp.where(diag, res2[d * N_PAIR + p], 0.0), axis=0, keepdims=True)
        for d in range(2):
            dirs[d][6][pl.ds(bases[d], SUBLANES), :] = ybuf[d]
        return carry

    lax.fori_loop(0, BLK // SUBLANES, group, 0)
    fin_ref[0] = st[...]


def _rwkv_scan(lay, r, v, a, wf, kf, bf, wb, kb, bb, init, e3):
    pb, bps = lay.pb, lay.bps

    def bwd_block(i):
        c = (i - pb) % bps
        return jnp.where(i < pb, i, i - c + (bps - 1 - c))

    fwd = pl.BlockSpec((BLK, D_B), lambda i: (i, 0))
    bwd = pl.BlockSpec((BLK, D_B), lambda i: (bwd_block(i), 0))
    st_shape = (2, N_PAIR, HEAD_B, LANES)
    y = jax.ShapeDtypeStruct((lay.rows, D_B), F32)
    return pl.pallas_call(
        functools.partial(_scan_kernel, lay=lay),
        grid=(lay.nb,),
        in_specs=[fwd] * 6 + [bwd] * 6
        + [pl.BlockSpec((1,) + st_shape, lambda i: (jnp.maximum(i - pb, 0) // bps, 0, 0, 0, 0)),
           pl.BlockSpec((N_SPLIT * LANES, LANES), lambda i: (0, 0))],
        out_specs=[fwd, bwd, pl.BlockSpec((1,) + st_shape, lambda i: (i, 0, 0, 0, 0))],
        out_shape=[y, y, jax.ShapeDtypeStruct((lay.nb,) + st_shape, F32)],
        scratch_shapes=[pltpu.VMEM(st_shape, F32), pltpu.VMEM((2, SUBLANES, D_B), F32)],
        compiler_params=_cparams("arbitrary"),
        name="rwkv_scan",
    )(r, v, a, wf, kf, bf, r, v, a, wb, kb, bb, init, e3)


def _post_kernel(yf_ref, yb_ref, bonus_ref, g_ref, lg_ref, lb_ref, e_ref, o_ref):
    for j in range(N_PAIR):
        sl = slice(j * LANES, (j + 1) * LANES)
        y = yf_ref[:, sl] + yb_ref[:, sl]
        mu = _seg_sum(y, e_ref) * (1.0 / HEAD_B)
        dlt = y - mu
        var = _seg_sum(dlt * dlt, e_ref) * (1.0 / HEAD_B)
        yn = dlt * lax.rsqrt(var + GN_EPS) * lg_ref[:, sl] + lb_ref[:, sl]
        o_ref[:, sl] = ((yn + bonus_ref[:, sl]) * g_ref[:, sl]).astype(o_ref.dtype)


def _rwkv_post(lay, yf, yb, bonus, g, lnx_g, lnx_b, e3):
    blk = pl.BlockSpec((BLK, D_B), lambda i: (i, 0))
    vec = pl.BlockSpec((1, D_B), lambda i: (0, 0))
    return pl.pallas_call(
        _post_kernel,
        grid=(lay.nb,),
        in_specs=[blk, blk, blk, blk, vec, vec, pl.BlockSpec((N_SPLIT * LANES, LANES), lambda i: (0, 0))],
        out_specs=blk,
        out_shape=jax.ShapeDtypeStruct((lay.rows, D_B), BF16),
        compiler_params=_cparams("parallel"),
        name="rwkv_post",
    )(yf, yb, bonus, g, lnx_g, lnx_b, e3)


def _lambda_value(dl_ref):
    dl = dl_ref[...]
    s01 = jnp.sum(dl[0:1] * dl[1:2], axis=-1, keepdims=True)
    s23 = jnp.sum(dl[2:3] * dl[3:4], axis=-1, keepdims=True)
    return jnp.exp(s01) - jnp.exp(s23) + LAM_INIT0


def _softmax_rows(s):
    e = jnp.exp(s - jnp.max(s, axis=-1, keepdims=True))
    return e / jnp.sum(e, axis=-1, keepdims=True)


def _diff_attend(q, k_bf, v_bf, lam, g_row):
    lane = lax.broadcasted_iota(jnp.int32, q.shape, 1)
    nt = (((1,), (1,)), ((), ()))
    q1 = jnp.where(lane < HEAD_QK, q, 0.0).astype(BF16)
    q2 = jnp.where(lane >= HEAD_QK, q, 0.0).astype(BF16)
    p1 = _softmax_rows(lax.dot_general(q1, k_bf, nt, preferred_element_type=F32))
    p2 = _softmax_rows(lax.dot_general(q2, k_bf, nt, preferred_element_type=F32))
    a = (p1 - lam * p2).astype(BF16)
    o = jnp.dot(a, v_bf, preferred_element_type=F32)
    o = o * lax.rsqrt(jnp.mean(o * o, axis=-1, keepdims=True) + LN_EPS) * g_row
    return o * (1.0 - LAM_INIT0)


def _attn_prompt_kernel(q_ref, k_ref, v_ref, dl_ref, g_ref, o_ref, ck_ref, cv_ref):
    k = k_ref[...]
    v = v_ref[...]
    ck_ref[0, 0, 0, 0] = k[:, :HEAD_QK]
    ck_ref[0, 0, 0, 1] = k[:, HEAD_QK:]
    cv_ref[0, 0, 0] = v
    o = _diff_attend(q_ref[...], (k * HEAD_QK ** -0.5).astype(BF16), v.astype(BF16),
                     _lambda_value(dl_ref), g_ref[...])
    o_ref[...] = o.astype(o_ref.dtype)


def _attn_prompt(lay, proj, diff_lambda, subln_g):
    qb, kb, vb = D_REST // LANES, (D_REST + D_A) // LANES, (D_REST + 2 * D_A) // LANES
    n = lay.n_prompt
    return pl.pallas_call(
        _attn_prompt_kernel,
        grid=(n, H_A),
        in_specs=[pl.BlockSpec((BLK, LANES), lambda b, h: (b, qb + h)),
                  pl.BlockSpec((BLK, LANES), lambda b, h: (b, kb + h)),
                  pl.BlockSpec((BLK, LANES), lambda b, h: (b, vb + h)),
                  pl.BlockSpec((4, HEAD_QK), lambda b, h: (0, 0)),
                  pl.BlockSpec((1, HEAD_V), lambda b, h: (0, 0))],
        out_specs=[pl.BlockSpec((BLK, LANES), lambda b, h: (b, h)),
                   pl.BlockSpec((1, 1, 1, 2, BLK, HEAD_QK), lambda b, h: (b, 0, h, 0, 0, 0)),
                   pl.BlockSpec((1, 1, 1, BLK, HEAD_V), lambda b, h: (b, 0, h, 0, 0))],
        out_shape=[jax.ShapeDtypeStruct((n * BLK, D_A), BF16),
                   jax.ShapeDtypeStruct((n, 1, H_A, 2, BLK, HEAD_QK), F32),
                   jax.ShapeDtypeStruct((n, 1, H_A, BLK, HEAD_V), F32)],
        compiler_params=_cparams("parallel", "parallel"),
        name="attn_prompt",
    )(proj, proj, proj, diff_lambda, subln_g.reshape(1, HEAD_V))


def _rope(x, cos, sin_signed):
    lane = lax.broadcasted_iota(jnp.int32, x.shape, 1)
    partner = jnp.where(lane % 32 < 16, pltpu.roll(x, LANES - 16, axis=1), pltpu.roll(x, 16, axis=1))
    return x * cos + partner * sin_signed


def _attn_sample_kernel(q_ref, k_ref, v_ref, ck_ref, cv_ref, cq_ref, sq_ref, ck_t_ref, sk_t_ref,
                        dl_ref, g_ref, o_ref, k_all, v_all, *, past, seq):
    @pl.when(pl.program_id(2) == 0)
    def _():
        scale = HEAD_QK ** -0.5
        k_all[0:past, :] = (jnp.concatenate([ck_ref[0, 0, 0, 0], ck_ref[0, 0, 0, 1]], axis=1)
                            * scale).astype(BF16)
        k_all[past:past + seq, :] = (_rope(k_ref[...], ck_t_ref[...], sk_t_ref[...]) * scale).astype(BF16)
        v_all[0:past, :] = cv_ref[0, 0, 0].astype(BF16)
        v_all[past:past + seq, :] = v_ref[...].astype(BF16)

    q = _rope(q_ref[...], cq_ref[...], sq_ref[...])
    o = _diff_attend(q, k_all[...], v_all[...], _lambda_value(dl_ref), g_ref[...])
    o_ref[...] = o.astype(o_ref.dtype)


def _attn_sample(lay, proj, cache_k, cache_v, cos_t, sin_t, diff_lambda, subln_g):
    qb, kb, vb = D_REST // LANES, (D_REST + D_A) // LANES, (D_REST + 2 * D_A) // LANES
    n, bps, pb = lay.n_sample, lay.bps, lay.pb
    seq = bps * BLK
    past = cache_k.shape[4]
    s0 = pb // bps
    assert pb % bps == 0
    return pl.pallas_call(
        functools.partial(_attn_sample_kernel, past=past, seq=seq),
        grid=(n, H_A, bps),
        in_specs=[pl.BlockSpec((BLK, LANES), lambda b, h, t: (pb + b * bps + t, qb + h)),
                  pl.BlockSpec((seq, LANES), lambda b, h, t: (s0 + b, kb + h)),
                  pl.BlockSpec((seq, LANES), lambda b, h, t: (s0 + b, vb + h)),
                  pl.BlockSpec((1, 1, 1, 2, past, HEAD_QK), lambda b, h, t: (b, 0, h, 0, 0, 0)),
                  pl.BlockSpec((1, 1, 1, past, HEAD_V), lambda b, h, t: (b, 0, h, 0, 0)),
                  pl.BlockSpec((BLK, LANES), lambda b, h, t: (t, 0)),
                  pl.BlockSpec((BLK, LANES), lambda b, h, t: (t, 0)),
                  pl.BlockSpec((seq, LANES), lambda b, h, t: (0, 0)),
                  pl.BlockSpec((seq, LANES), lambda b, h, t: (0, 0)),
                  pl.BlockSpec((4, HEAD_QK), lambda b, h, t: (0, 0)),
                  pl.BlockSpec((1, HEAD_V), lambda b, h, t: (0, 0))],
        out_specs=pl.BlockSpec((BLK, LANES), lambda b, h, t: (b * bps + t, h)),
        out_shape=jax.ShapeDtypeStruct((n * seq, D_A), BF16),
        scratch_shapes=[pltpu.VMEM((past + seq, LANES), BF16), pltpu.VMEM((past + seq, LANES), BF16)],
        compiler_params=_cparams("parallel", "parallel", "arbitrary"),
        name="attn_sample",
    )(proj, proj, proj, cache_k, cache_v, cos_t, sin_t, cos_t, sin_t, diff_lambda,
      subln_g.reshape(1, HEAD_V))


def _rope_tables(seq):
    t = jnp.arange(seq)
    lane = jnp.arange(LANES)
    n_freq = HEAD_QK // 4
    inv = ROPE_BASE ** (-(lane % n_freq).astype(F32) / n_freq)
    use_col = (lane % HEAD_QK) >= HEAD_QK // 2
    pos = jnp.where(use_col[None, :], (t % GRID_W)[:, None], (t // GRID_W)[:, None]).astype(F32)
    ang = pos * inv[None, :]
    sin_signed = jnp.where((lane % (2 * n_freq) < n_freq)[None, :], -jnp.sin(ang), jnp.sin(ang))
    return jnp.cos(ang), sin_signed


CONV_ROWS = 32
CONV_LANES = 512


def _conv_ln_kernel(u_ref, up_ref, un_ref, w_ref, b_ref, g_ref, beta_ref, o_ref, pad, tmp, *, lay):
    i = pl.program_id(0)
    first, last = lay.seq_pos(i)
    d = u_ref.shape[1]
    n_ct = d // CONV_LANES
    for ct in range(n_ct):
        sl = slice(ct * CONV_LANES, (ct + 1) * CONV_LANES)
        pad[ct, 0:CONV_HALO, :] = jnp.where(first, 0.0, up_ref[:, sl])
        pad[ct, CONV_HALO:CONV_HALO + BLK, :] = u_ref[:, sl]
        pad[ct, CONV_HALO + BLK:, :] = jnp.where(last, 0.0, un_ref[:, sl])
    off = CONV_HALO - CONV_PAD

    def conv_tile(ct, carry):
        for c in range(BLK // CONV_ROWS):
            r0 = c * CONV_ROWS
            acc = jnp.zeros((CONV_ROWS, CONV_LANES), F32) + b_ref[ct]
            for j in range(CONV_W):
                acc = acc + pad[ct, r0 + off + j:r0 + off + j + CONV_ROWS, :] * w_ref[ct, j:j + 1, :]
            tmp[ct, r0:r0 + CONV_ROWS, :] = acc
        return carry

    lax.fori_loop(0, n_ct, conv_tile, 0)

    def norm_rows(c, carry):
        r0 = pl.multiple_of(c * CONV_ROWS, CONV_ROWS)
        xs = [tmp[ct, pl.ds(r0, CONV_ROWS), :] for ct in range(n_ct)]
        mu = sum(jnp.sum(x, axis=-1, keepdims=True) for x in xs) * (1.0 / d)
        xs = [x - mu for x in xs]
        var = sum(jnp.sum(x * x, axis=-1, keepdims=True) for x in xs) * (1.0 / d)
        inv = lax.rsqrt(var + LN_EPS)
        for ct, x in enumerate(xs):
            sl = slice(ct * CONV_LANES, (ct + 1) * CONV_LANES)
            y = x * inv * g_ref[:, sl] + beta_ref[:, sl]
            o_ref[pl.ds(r0, CONV_ROWS), sl] = (y * jax.nn.sigmoid(y)).astype(o_ref.dtype)
        return carry

    lax.fori_loop(0, BLK // CONV_ROWS, norm_rows, 0)


def _conv_ln(lay, u, w_dw, b_dw, cln_g, cln_b):
    d = u.shape[1]
    hb = BLK // CONV_HALO
    n_halo = lay.rows // CONV_HALO
    n_ct = d // CONV_LANES
    vec = pl.BlockSpec((1, d), lambda i: (0, 0))
    w_t = w_dw.reshape(CONV_W, n_ct, CONV_LANES).transpose(1, 0, 2)
    return pl.pallas_call(
        functools.partial(_conv_ln_kernel, lay=lay),
        grid=(lay.nb,),
        in_specs=[pl.BlockSpec((BLK, d), lambda i: (i, 0)),
                  pl.BlockSpec((CONV_HALO, d), lambda i: (jnp.maximum(i * hb - 1, 0), 0)),
                  pl.BlockSpec((CONV_HALO, d), lambda i: (jnp.minimum((i + 1) * hb, n_halo - 1), 0)),
                  pl.BlockSpec((n_ct, CONV_W, CONV_LANES), lambda i: (0, 0, 0)),
                  pl.BlockSpec((n_ct, 1, CONV_LANES), lambda i: (0, 0, 0)), vec, vec],
        out_specs=pl.BlockSpec((BLK, d), lambda i: (i, 0)),
        out_shape=jax.ShapeDtypeStruct(u.shape, BF16),
        scratch_shapes=[pltpu.VMEM((n_ct, BLK + 2 * CONV_HALO, CONV_LANES), F32),
                        pltpu.VMEM((n_ct, BLK, CONV_LANES), F32)],
        compiler_params=_cparams("parallel"),
        name="conv_ln",
    )(u, u, u, w_t, b_dw.reshape(n_ct, 1, CONV_LANES), cln_g.reshape(1, d), cln_b.reshape(1, d))


def _pad_cols(x, n):
    return jnp.pad(x, ((0, 0), (0, n - x.shape[1])))


def _rest_layout(x):
    c = 3 * D_B
    return jnp.concatenate([x[:, :c], _pad_cols(x[:, c:c + LORA_W], LANES),
                            _pad_cols(x[:, c + LORA_W:c + LORA_W + LORA_A], LANES),
                            x[:, c + LORA_W + LORA_A:]], axis=1)


def _ffn_weights(w_up, w_dw, w_down):
    padc = lambda x: jnp.concatenate([_pad_cols(x[:, :D_FF], D_FF_PAD), _pad_cols(x[:, D_FF:], D_FF_PAD)], axis=1)
    return (padc(w_up).astype(BF16), padc(w_dw),
            jnp.pad(w_down, ((0, D_FF_PAD - D_FF), (0, 0))).astype(BF16))


def _pair_state(s):
    b = s.shape[0]
    return s.reshape(b, N_PAIR, 2, HEAD_B, HEAD_B).transpose(0, 1, 3, 2, 4).reshape(b, N_PAIR, HEAD_B, LANES)


def _unpair_state(s):
    b = s.shape[0]
    return s.reshape(b, N_PAIR, HEAD_B, 2, HEAD_B).transpose(0, 1, 3, 2, 4).reshape(b, H_B, HEAD_B, HEAD_B)


def kernel(x_prompt, x_sample, cache_k, cache_v, state_wkv_fwd, state_wkv_bwd, c, c_ctx, w_ada, b_ada, norm_g, final_norm_g, w_in, w_out, diff_lambda, subln_g, shift_mu, w0, w2, a0, a2, g2, k_k, k_a, r_k, lnx_g, lnx_b, w_pw1, w_dw, b_dw, cln_g, cln_b, w_pw2, w_up, w_ffn_dw, w_down):
    n_prompt, seq_p, d = x_prompt.shape
    n_sample, seq_s, _ = x_sample.shape
    assert seq_p == BLK and seq_s % BLK == 0 and d == D_MODEL
    lay = _Layout(n_prompt, n_sample, seq_s)
    tm = lay.tile_rows()
    tm_small = min(tm, 512)

    x = jnp.concatenate([x_prompt.reshape(-1, d), x_sample.reshape(-1, d)], axis=0)
    c16 = jnp.concatenate([c_ctx[None, :], c, jnp.zeros((N_COND - 1 - n_sample, d), F32)], axis=0)
    mod = _adaln(c16, w_ada, b_ada)
    mod = mod.reshape(mod.shape[0], N_COND, 1, 6 * d)
    e3 = _seg_ones()

    e = 0
    w_rest = w_in[e][:, 2 * D_A + D_A:]
    w_in_p = jnp.concatenate([_rest_layout(w_rest), w_in[e][:, :3 * D_A]], axis=1).astype(BF16)
    mu_p = _rest_layout(shift_mu[e])
    pad_rows = lambda w: jnp.pad(w, ((0, 0), (0, LANES - w.shape[1]), (0, 0))).astype(BF16)

    h = _norm_mod(lay, x, norm_g[0, 0], mod[0], 0)
    proj = _mm(h, w_in_p, tm, 512)
    oa_p, new_k, new_v = _attn_prompt(lay, proj, diff_lambda[e], subln_g[e])
    cos_t, sin_t = _rope_tables(seq_s)
    oa_s = _attn_sample(lay, proj, cache_k[:, e:e + 1], cache_v[:, e:e + 1], cos_t, sin_t,
                        diff_lambda[e], subln_g[e])
    o_a = jnp.concatenate([oa_p, oa_s], axis=0)

    r_, v_, a_, wf, kf, bf, wb, kb, bb, g_, bonus = _rwkv_prep(
        lay, proj, mu_p, w0[e], pad_rows(w2[e]), a0[e], pad_rows(a2[e]), g2[e].astype(BF16),
        k_k[e].reshape(1, D_B), k_a[e].reshape(1, D_B), r_k[e].reshape(1, D_B), e3)
    init = jnp.stack([_pair_state(state_wkv_fwd[:, e]), _pair_state(state_wkv_bwd[:, e])], axis=1)
    yf, yb, fin = _rwkv_scan(lay, r_, v_, a_, wf, kf, bf, wb, kb, bb, init, e3)
    o_b = _rwkv_post(lay, yf, yb, bonus, g_, lnx_g[e].reshape(1, D_B), lnx_b[e].reshape(1, D_B), e3)
    new_sf = _unpair_state(fin[:n_prompt, 0])[:, None]
    new_sb = _unpair_state(fin[:n_prompt, 1])[:, None]

    w_out_b = w_out[e].astype(BF16)
    x = _mm_res(lay, [o_a, o_b], [w_out_b[:D_A], w_out_b[D_A:]], x, mod[0], 2, tm_small, 512)
    wu, wd, wdn = _ffn_weights(w_up[0], w_ffn_dw[0], w_down[0])
    h = _norm_mod(lay, x, norm_g[0, 1], mod[0], 3)
    u = _ffn_up(lay, h, wu, wd, tm, 512)
    x = _mm_res(lay, [u], [wdn], x, mod[0], 5, tm_small, 512)

    o = 0
    h = _norm_mod(lay, x, norm_g[1, 0], mod[1], 0)
    u = _mm_glu(h, w_pw1[o].astype(BF16), tm, 512)
    u = _conv_ln(lay, u, w_dw[o], b_dw[o], cln_g[o], cln_b[o])
    x = _mm_res(lay, [u], [w_pw2[o].astype(BF16)], x, mod[1], 2, tm_small, 512)
    wu, wd, wdn = _ffn_weights(w_up[1], w_ffn_dw[1], w_down[1])
    h = _norm_mod(lay, x, norm_g[1, 1], mod[1], 3)
    u = _ffn_up(lay, h, wu, wd, tm, 512)
    x = _mm_res(lay, [u], [wdn], x, mod[1], 5, tm_small, 512)

    y_p = _final_norm(x, final_norm_g, 0, lay.pb).reshape(n_prompt, seq_p, d)
    y_s = _final_norm(x, final_norm_g, lay.pb, lay.nb - lay.pb).reshape(n_sample, seq_s, d)
    return (y_p, y_s, new_k, new_v, new_sf, new_sb)
```

```python
import functools
import math

import jax
import jax.numpy as jnp
from jax import lax
from jax.experimental import pallas as pl
from jax.experimental.pallas import tpu as pltpu

F32 = jnp.float32
BF16 = jnp.bfloat16

LANES = 128
SUBLANES = 8
VMEM_LIMIT_BYTES = 56 * 1024 * 1024

D_MODEL = 2048
BLK = 256
GRID_W = 64
HEAD_QK = 64
HEAD_V = 128
H_A = 8
D_A = 1024
HEAD_B = 64
H_B = 16
D_B = 1024
N_PAIR = D_B // LANES
LORA_W = 96
LORA_A = 96
LORA_G = 256
D_REST = 3 * D_B + 2 * LANES + LORA_G
D_PROJ = D_REST + 3 * D_A
D_FF = 5504
D_FF_PAD = 5632
CONV_W = 31
CONV_PAD = (CONV_W - 1) // 2
CONV_HALO = 16
RMS_EPS = 1e-6
LN_EPS = 1e-5
GN_EPS = 64e-5
ROPE_BASE = 10000.0
LAM_INIT0 = 0.8 - 0.6 * math.exp(-0.3 * 0)
N_SPLIT = 2
N_COND = 16


def _cparams(*sem):
    return pltpu.CompilerParams(dimension_semantics=sem, vmem_limit_bytes=VMEM_LIMIT_BYTES)


class _Layout:
    def __init__(self, n_prompt, n_sample, sample_len):
        self.n_prompt = n_prompt
        self.n_sample = n_sample
        self.bps = sample_len // BLK
        self.pb = n_prompt
        self.nb = n_prompt + n_sample * self.bps
        self.rows = self.nb * BLK
        assert 1 + n_sample <= N_COND

    def cond_of_block(self, i):
        return jnp.where(i < self.pb, 0, 1 + (i - self.pb) // self.bps)

    def seq_pos(self, i):
        c = (i - self.pb) % self.bps
        first = jnp.logical_or(i < self.pb, c == 0)
        last = jnp.logical_or(i < self.pb, c == self.bps - 1)
        return first, last

    def tile_rows(self):
        t = self.bps * BLK
        while (self.pb * BLK) % t:
            t //= 2
        return t


def _adaln_kernel(c_ref, w_ref, b_ref, o_ref):
    x = c_ref[...]
    s = (x * jax.nn.sigmoid(x)).astype(BF16)
    o_ref[0] = jnp.dot(s, w_ref[0].astype(BF16), preferred_element_type=F32) + b_ref[0]


def _adaln(c16, w_ada, b_ada):
    depth, d, n = w_ada.shape
    tn = 1024
    return pl.pallas_call(
        _adaln_kernel,
        grid=(depth, n // tn),
        in_specs=[pl.BlockSpec((N_COND, d), lambda l, j: (0, 0)),
                  pl.BlockSpec((1, d, tn), lambda l, j: (l, 0, j)),
                  pl.BlockSpec((1, 1, tn), lambda l, j: (l, 0, j))],
        out_specs=pl.BlockSpec((1, N_COND, tn), lambda l, j: (l, 0, j)),
        out_shape=jax.ShapeDtypeStruct((depth, N_COND, n), F32),
        compiler_params=_cparams("parallel", "parallel"),
        name="adaln",
    )(c16, w_ada, b_ada.reshape(depth, 1, n))


def _norm_mod_kernel(x_ref, g_ref, sh_ref, sc_ref, o_ref):
    x = x_ref[...]
    y = x * lax.rsqrt(jnp.mean(x * x, axis=-1, keepdims=True) + RMS_EPS) * g_ref[...]
    o_ref[...] = (y * (1.0 + sc_ref[0]) + sh_ref[0]).astype(o_ref.dtype)


def _norm_mod(lay, x, g, mod, shift_idx):
    d = x.shape[1]
    cond = lay.cond_of_block
    return pl.pallas_call(
        _norm_mod_kernel,
        grid=(lay.nb,),
        in_specs=[pl.BlockSpec((BLK, d), lambda i: (i, 0)),
                  pl.BlockSpec((1, d), lambda i: (0, 0)),
                  pl.BlockSpec((1, 1, d), lambda i: (cond(i), 0, shift_idx)),
                  pl.BlockSpec((1, 1, d), lambda i: (cond(i), 0, shift_idx + 1))],
        out_specs=pl.BlockSpec((BLK, d), lambda i: (i, 0)),
        out_shape=jax.ShapeDtypeStruct(x.shape, BF16),
        compiler_params=_cparams("parallel"),
        name="norm_mod",
    )(x, g.reshape(1, d), mod, mod)


def _final_norm_kernel(x_ref, g_ref, o_ref):
    x = x_ref[...]
    o_ref[...] = x * lax.rsqrt(jnp.mean(x * x, axis=-1, keepdims=True) + RMS_EPS) * g_ref[...]


def _final_norm(x, g, blk0, nblk):
    d = x.shape[1]
    return pl.pallas_call(
        _final_norm_kernel,
        grid=(nblk,),
        in_specs=[pl.BlockSpec((BLK, d), lambda i: (i + blk0, 0)),
                  pl.BlockSpec((1, d), lambda i: (0, 0))],
        out_specs=pl.BlockSpec((BLK, d), lambda i: (i, 0)),
        out_shape=jax.ShapeDtypeStruct((nblk * BLK, d), F32),
        compiler_params=_cparams("parallel"),
        name="final_norm",
    )(x, g.reshape(1, d))


def _mm_kernel(a_ref, w_ref, o_ref):
    o_ref[...] = jnp.dot(a_ref[...], w_ref[...], preferred_element_type=F32)


def _mm(a, w, tm, tn):
    m, k = a.shape
    n = w.shape[1]
    return pl.pallas_call(
        _mm_kernel,
        grid=(m // tm, n // tn),
        in_specs=[pl.BlockSpec((tm, k), lambda i, j: (i, 0)),
                  pl.BlockSpec((k, tn), lambda i, j: (0, j))],
        out_specs=pl.BlockSpec((tm, tn), lambda i, j: (i, j)),
        out_shape=jax.ShapeDtypeStruct((m, n), F32),
        compiler_params=_cparams("parallel", "parallel"),
        name="mm",
    )(a, w)


def _mm_res_kernel(*refs, n_in):
    a_refs, w_refs = refs[:n_in], refs[n_in:2 * n_in]
    x_ref, g_ref, o_ref = refs[2 * n_in:]
    acc = jnp.dot(a_refs[0][...], w_refs[0][...], preferred_element_type=F32)
    for a_ref, w_ref in zip(a_refs[1:], w_refs[1:]):
        acc += jnp.dot(a_ref[...], w_ref[...], preferred_element_type=F32)
    o_ref[...] = x_ref[...] + g_ref[0] * acc


def _mm_res(lay, a_list, w_list, x, mod, gate_idx, tm, tn):
    m, n = x.shape
    bpt = tm // BLK
    cond = lambda i: lay.cond_of_block(i * bpt)
    in_specs = ([pl.BlockSpec((tm, a.shape[1]), lambda i, j: (i, 0)) for a in a_list]
                + [pl.BlockSpec((w.shape[0], tn), lambda i, j: (0, j)) for w in w_list]
                + [pl.BlockSpec((tm, tn), lambda i, j: (i, j)),
                   pl.BlockSpec((1, 1, tn), lambda i, j: (cond(i), 0, gate_idx * (n // tn) + j))])
    return pl.pallas_call(
        functools.partial(_mm_res_kernel, n_in=len(a_list)),
        grid=(m // tm, n // tn),
        in_specs=in_specs,
        out_specs=pl.BlockSpec((tm, tn), lambda i, j: (i, j)),
        out_shape=jax.ShapeDtypeStruct((m, n), F32),
        compiler_params=_cparams("parallel", "parallel"),
        name="mm_res",
    )(*a_list, *w_list, x, mod)


def _mm_glu_kernel(a_ref, wa_ref, wb_ref, o_ref):
    a = a_ref[...]
    u = jnp.dot(a, wa_ref[...], preferred_element_type=F32)
    v = jnp.dot(a, wb_ref[...], preferred_element_type=F32)
    o_ref[...] = u * jax.nn.sigmoid(v)


def _mm_glu(a, w, tm, tn):
    m, k = a.shape
    n = w.shape[1] // 2
    nj = n // tn
    return pl.pallas_call(
        _mm_glu_kernel,
        grid=(m // tm, nj),
        in_specs=[pl.BlockSpec((tm, k), lambda i, j: (i, 0)),
                  pl.BlockSpec((k, tn), lambda i, j: (0, j)),
                  pl.BlockSpec((k, tn), lambda i, j: (0, j + nj))],
        out_specs=pl.BlockSpec((tm, tn), lambda i, j: (i, j)),
        out_shape=jax.ShapeDtypeStruct((m, n), F32),
        compiler_params=_cparams("parallel", "parallel"),
        name="mm_glu",
    )(a, w, w)


def _row_index(shape):
    return lax.broadcasted_iota(jnp.int32, shape, 0)


def _ffn_up_kernel(a_ref, wg_ref, wv_ref, dg_ref, dv_ref, o_ref, *, lay, tm):
    i = pl.program_id(0)
    seq_len = jnp.where(i * (tm // BLK) < lay.pb, BLK, lay.bps * BLK)
    a = a_ref[...]

    def conv(w_ref, d_ref):
        u = jnp.dot(a, w_ref[...], preferred_element_type=F32)
        pos = _row_index(u.shape) & (seq_len - 1)
        prev = jnp.where(pos == 0, 0.0, pltpu.roll(u, 1, axis=0))
        nxt = jnp.where(pos == seq_len - 1, 0.0, pltpu.roll(u, tm - 1, axis=0))
        return prev * d_ref[0:1, :] + u * d_ref[1:2, :] + nxt * d_ref[2:3, :]

    gate = conv(wg_ref, dg_ref)
    val = conv(wv_ref, dv_ref)
    o_ref[...] = (gate * jax.nn.sigmoid(gate) * val).astype(o_ref.dtype)


def _ffn_up(lay, a, w_up, w_dw, tm, tn):
    m, k = a.shape
    f = w_up.shape[1] // 2
    nj = f // tn
    return pl.pallas_call(
        functools.partial(_ffn_up_kernel, lay=lay, tm=tm),
        grid=(m // tm, nj),
        in_specs=[pl.BlockSpec((tm, k), lambda i, j: (i, 0)),
                  pl.BlockSpec((k, tn), lambda i, j: (0, j)),
                  pl.BlockSpec((k, tn), lambda i, j: (0, j + nj)),
                  pl.BlockSpec((3, tn), lambda i, j: (0, j)),
                  pl.BlockSpec((3, tn), lambda i, j: (0, j + nj))],
        out_specs=pl.BlockSpec((tm, tn), lambda i, j: (i, j)),
        out_shape=jax.ShapeDtypeStruct((m, f), BF16),
        compiler_params=_cparams("parallel", "parallel"),
        name="ffn_up",
    )(a, w_up, w_up, w_dw, w_dw)


def _split_bf16(x):
    parts = []
    r = x
    for s in range(N_SPLIT):
        p = r.astype(BF16)
        parts.append(p)
        if s + 1 < N_SPLIT:
            r = r - p.astype(F32)
    return parts


def _seg_sum(x, e_ref):
    lhs = jnp.concatenate(_split_bf16(x), axis=1)
    return jnp.dot(lhs, e_ref[...], preferred_element_type=F32)


def _seg_ones():
    r = jnp.arange(N_SPLIT * LANES)[:, None] % LANES
    c = jnp.arange(LANES)[None, :]
    return (r // HEAD_B == c // HEAD_B).astype(BF16)


def _seg_ones_pair():
    r = jnp.arange(2 * LANES)[:, None]
    c = jnp.arange(2 * LANES)[None, :]
    return (r // HEAD_B == c // HEAD_B).astype(BF16)


def _softplus(z):
    return jnp.maximum(z, 0.0) + jnp.log(1.0 + jnp.exp(-jnp.abs(z)))


def _prep_kernel(p_ref, pp_ref, pn_ref, mu_ref, w0_ref, w2_ref, a0_ref, a2_ref, g2_ref,
                 kk_ref, ka_ref, rk_ref, e_ref,
                 r_o, v_o, a_o, wf_o, kf_o, bf_o, wb_o, kb_o, bb_o, g_o, bonus_o, *, lay):
    i = pl.program_id(0)
    first, last = lay.seq_pos(i)
    rows = _row_index((BLK, 1))

    def shifted(c0, c1):
        p = p_ref[:, c0:c1]
        prev_row = jnp.where(first, 0.0, pp_ref[SUBLANES - 1:SUBLANES, c0:c1])
        next_row = jnp.where(last, 0.0, pn_ref[0:1, c0:c1])
        p_prev = jnp.where(rows == 0, prev_row, pltpu.roll(p, 1, axis=0))
        p_next = jnp.where(rows == BLK - 1, next_row, pltpu.roll(p, BLK - 1, axis=0))
        return p + mu_ref[0:1, c0:c1] * (p_prev - p) + mu_ref[1:2, c0:c1] * (p_next - p)

    c_w = 3 * D_B
    c_a = c_w + LANES
    c_g = c_a + LANES
    tw = jnp.tanh(shifted(c_w, c_a)).astype(BF16)
    xa = shifted(c_a, c_g).astype(BF16)
    sg = jax.nn.sigmoid(shifted(c_g, D_REST)).astype(BF16)

    for j in range(N_PAIR):
        sl = slice(j * LANES, (j + 1) * LANES)
        r = shifted(j * LANES, (j + 1) * LANES)
        kb = shifted(D_B + j * LANES, D_B + (j + 1) * LANES)
        vb = shifted(2 * D_B + j * LANES, 2 * D_B + (j + 1) * LANES)
        r_o[:, sl] = r
        v_o[:, sl] = vb
        g_o[:, sl] = jnp.dot(sg, g2_ref[:, sl], preferred_element_type=F32)
        kk = kb * kk_ref[:, sl]
        nrm = jnp.sqrt(_seg_sum(kk * kk, e_ref))
        kk = kk / jnp.maximum(nrm, 1e-12)
        a_o[:, sl] = -kk
        bonus_o[:, sl] = _seg_sum(r * kb * rk_ref[:, sl], e_ref) * vb
        for d, (w_o, k_o, b_o) in enumerate(((wf_o, kf_o, bf_o), (wb_o, kb_o, bb_o))):
            lw = w0_ref[d:d + 1, sl] + jnp.dot(tw, w2_ref[d, :, sl], preferred_element_type=F32)
            w_log = -_softplus(-lw) - 0.5
            w_o[:, sl] = jnp.exp(-jnp.exp(w_log))
            a_lr = jax.nn.sigmoid(a0_ref[d:d + 1, sl]
                                  + jnp.dot(xa, a2_ref[d, :, sl], preferred_element_type=F32))
            k_o[:, sl] = kb * (1.0 + (a_lr - 1.0) * ka_ref[:, sl])
            b_o[:, sl] = kk * a_lr


def _rwkv_prep(lay, proj, mu, w0, w2p, a0, a2p, g2, k_k, k_a, r_k, e3):
    nb = lay.nb
    hb = BLK // SUBLANES
    n_halo = lay.rows // SUBLANES
    full = lambda shape: pl.BlockSpec(shape, lambda i: (0,) * len(shape))
    out = jax.ShapeDtypeStruct((lay.rows, D_B), F32)
    return pl.pallas_call(
        functools.partial(_prep_kernel, lay=lay),
        grid=(nb,),
        in_specs=[pl.BlockSpec((BLK, D_REST), lambda i: (i, 0)),
                  pl.BlockSpec((SUBLANES, D_REST), lambda i: (jnp.maximum(i * hb - 1, 0), 0)),
                  pl.BlockSpec((SUBLANES, D_REST), lambda i: (jnp.minimum((i + 1) * hb, n_halo - 1), 0)),
                  full((2, D_REST)), full((2, D_B)), full((2, LANES, D_B)), full((2, D_B)),
                  full((2, LANES, D_B)), full((LORA_G, D_B)), full((1, D_B)), full((1, D_B)),
                  full((1, D_B)), full((N_SPLIT * LANES, LANES))],
        out_specs=[pl.BlockSpec((BLK, D_B), lambda i: (i, 0))] * 11,
        out_shape=[out] * 11,
        compiler_params=_cparams("parallel"),
        name="rwkv_prep",
    )(proj, proj, proj, mu, w0, w2p, a0, a2p, g2, k_k, k_a, r_k, e3)


def _scan_kernel(rf, vf, af, wf, kf, bf, rb, vb, ab, wb, kb, bb, init_ref, e_ref,
                 yf_ref, yb_ref, fin_ref, st, ybuf, *, lay):
    i = pl.program_id(0)
    first, _ = lay.seq_pos(i)

    @pl.when(first)
    def _():
        st[...] = jnp.where(i >= lay.pb, init_ref[0], 0.0)

    diag = (lax.broadcasted_iota(jnp.int32, (HEAD_B, LANES), 0)
            == lax.broadcasted_iota(jnp.int32, (HEAD_B, LANES), 1) % HEAD_B)
    dirs = ((rf, vf, af, wf, kf, bf, yf_ref), (rb, vb, ab, wb, kb, bb, yb_ref))

    def seg_sums(tiles):
        bt = [t.astype(BF16) for t in tiles]
        lhs = jnp.concatenate([jnp.concatenate(bt[k:k + 2], axis=1) for k in range(0, len(bt), 2)], axis=0)
        res = jnp.dot(lhs, e_ref[...], preferred_element_type=F32)
        return [res[(k // 2) * HEAD_B:(k // 2 + 1) * HEAD_B, (k % 2) * LANES:(k % 2 + 1) * LANES]
                for k in range(len(bt))]

    def group(gi, carry):
        bases = (pl.multiple_of(gi * SUBLANES, SUBLANES),
                 pl.multiple_of(BLK - SUBLANES - gi * SUBLANES, SUBLANES))

        def row(ref, d, j, sl):
            return ref[pl.ds(bases[d], SUBLANES), sl][j:j + 1, :]

        def emit_y(js, tiles):
            for d in range(2):
                for p in range(N_PAIR):
                    ybuf[d, js[d]:js[d] + 1, p * LANES:(p + 1) * LANES] = jnp.sum(
                        jnp.where(diag, tiles[d * N_PAIR + p], 0.0), axis=0, keepdims=True)

        js_prev = None
        for t in range(SUBLANES):
            js = (t, SUBLANES - 1 - t)
            xs, ys = [], []
            for d in range(2):
                r_r, v_r, a_r = dirs[d][0], dirs[d][1], dirs[d][2]
                for p in range(N_PAIR):
                    sl = slice(p * LANES, (p + 1) * LANES)
                    s_cur = st[d, p]
                    xs.append(s_cur * row(a_r, d, js[d], sl))
                    xs.append(jnp.where(diag, row(v_r, d, js[d], sl), 0.0))
                    if js_prev is not None:
                        ys.append(s_cur * row(r_r, d, js_prev[d], sl))
            res = seg_sums(xs + ys)
            if js_prev is not None:
                emit_y(js_prev, res[len(xs):])
            for d in range(2):
                w_r, k_r, b_r = dirs[d][3], dirs[d][4], dirs[d][5]
                for p in range(N_PAIR):
                    sl = slice(p * LANES, (p + 1) * LANES)
                    idx = d * N_PAIR + p
                    sa, vcol = res[2 * idx], res[2 * idx + 1]
                    st[d, p] = (st[d, p] * row(w_r, d, js[d], sl) + sa * row(b_r, d, js[d], sl)
                                + vcol * row(k_r, d, js[d], sl))
            js_prev = js
        emit_y(js_prev, seg_sums([st[d, p] * row(dirs[d][0], d, js_prev[d], slice(p * LANES, (p + 1) * LANES))
                                  for d in range(2) for p in range(N_PAIR)]))
        for d in range(2):
            dirs[d][6][pl.ds(bases[d], SUBLANES), :] = ybuf[d]
        return carry

    lax.fori_loop(0, BLK // SUBLANES, group, 0)
    fin_ref[0] = st[...]


def _rwkv_scan(lay, r, v, a, wf, kf, bf, wb, kb, bb, init, e3):
    pb, bps = lay.pb, lay.bps

    def bwd_block(i):
        c = (i - pb) % bps
        return jnp.where(i < pb, i, i - c + (bps - 1 - c))

    fwd = pl.BlockSpec((BLK, D_B), lambda i: (i, 0))
    bwd = pl.BlockSpec((BLK, D_B), lambda i: (bwd_block(i), 0))
    st_shape = (2, N_PAIR, HEAD_B, LANES)
    y = jax.ShapeDtypeStruct((lay.rows, D_B), F32)
    return pl.pallas_call(
        functools.partial(_scan_kernel, lay=lay),
        grid=(lay.nb,),
        in_specs=[fwd] * 6 + [bwd] * 6
        + [pl.BlockSpec((1,) + st_shape, lambda i: (jnp.maximum(i - pb, 0) // bps, 0, 0, 0, 0)),
           pl.BlockSpec((2 * LANES, 2 * LANES), lambda i: (0, 0))],
        out_specs=[fwd, bwd, pl.BlockSpec((1,) + st_shape, lambda i: (i, 0, 0, 0, 0))],
        out_shape=[y, y, jax.ShapeDtypeStruct((lay.nb,) + st_shape, F32)],
        scratch_shapes=[pltpu.VMEM(st_shape, F32), pltpu.VMEM((2, SUBLANES, D_B), F32)],
        compiler_params=_cparams("arbitrary"),
        name="rwkv_scan",
    )(r, v, a, wf, kf, bf, r, v, a, wb, kb, bb, init, e3)


def _post_kernel(yf_ref, yb_ref, bonus_ref, g_ref, lg_ref, lb_ref, e_ref, o_ref):
    for j in range(N_PAIR):
        sl = slice(j * LANES, (j + 1) * LANES)
        y = yf_ref[:, sl] + yb_ref[:, sl]
        mu = _seg_sum(y, e_ref) * (1.0 / HEAD_B)
        dlt = y - mu
        var = _seg_sum(dlt * dlt, e_ref) * (1.0 / HEAD_B)
        yn = dlt * lax.rsqrt(var + GN_EPS) * lg_ref[:, sl] + lb_ref[:, sl]
        o_ref[:, sl] = ((yn + bonus_ref[:, sl]) * g_ref[:, sl]).astype(o_ref.dtype)


def _rwkv_post(lay, yf, yb, bonus, g, lnx_g, lnx_b, e3):
    blk = pl.BlockSpec((BLK, D_B), lambda i: (i, 0))
    vec = pl.BlockSpec((1, D_B), lambda i: (0, 0))
    return pl.pallas_call(
        _post_kernel,
        grid=(lay.nb,),
        in_specs=[blk, blk, blk, blk, vec, vec, pl.BlockSpec((N_SPLIT * LANES, LANES), lambda i: (0, 0))],
        out_specs=blk,
        out_shape=jax.ShapeDtypeStruct((lay.rows, D_B), BF16),
        compiler_params=_cparams("parallel"),
        name="rwkv_post",
    )(yf, yb, bonus, g, lnx_g, lnx_b, e3)


def _lambda_value(dl_ref):
    dl = dl_ref[...]
    s01 = jnp.sum(dl[0:1] * dl[1:2], axis=-1, keepdims=True)
    s23 = jnp.sum(dl[2:3] * dl[3:4], axis=-1, keepdims=True)
    return jnp.exp(s01) - jnp.exp(s23) + LAM_INIT0


def _softmax_rows(s):
    e = jnp.exp(s - jnp.max(s, axis=-1, keepdims=True))
    return e / jnp.sum(e, axis=-1, keepdims=True)


def _diff_attend(q, k_bf, v_bf, lam, g_row):
    lane = lax.broadcasted_iota(jnp.int32, q.shape, 1)
    nt = (((1,), (1,)), ((), ()))
    q1 = jnp.where(lane < HEAD_QK, q, 0.0).astype(BF16)
    q2 = jnp.where(lane >= HEAD_QK, q, 0.0).astype(BF16)
    p1 = _softmax_rows(lax.dot_general(q1, k_bf, nt, preferred_element_type=F32))
    p2 = _softmax_rows(lax.dot_general(q2, k_bf, nt, preferred_element_type=F32))
    a = (p1 - lam * p2).astype(BF16)
    o = jnp.dot(a, v_bf, preferred_element_type=F32)
    o = o * lax.rsqrt(jnp.mean(o * o, axis=-1, keepdims=True) + LN_EPS) * g_row
    return o * (1.0 - LAM_INIT0)


def _attn_prompt_kernel(q_ref, k_ref, v_ref, dl_ref, g_ref, o_ref, ck_ref, cv_ref):
    k = k_ref[...]
    v = v_ref[...]
    ck_ref[0, 0, 0, 0] = k[:, :HEAD_QK]
    ck_ref[0, 0, 0, 1] = k[:, HEAD_QK:]
    cv_ref[0, 0, 0] = v
    o = _diff_attend(q_ref[...], (k * HEAD_QK ** -0.5).astype(BF16), v.astype(BF16),
                     _lambda_value(dl_ref), g_ref[...])
    o_ref[...] = o.astype(o_ref.dtype)


def _attn_prompt(lay, proj, diff_lambda, subln_g):
    qb, kb, vb = D_REST // LANES, (D_REST + D_A) // LANES, (D_REST + 2 * D_A) // LANES
    n = lay.n_prompt
    return pl.pallas_call(
        _attn_prompt_kernel,
        grid=(n, H_A),
        in_specs=[pl.BlockSpec((BLK, LANES), lambda b, h: (b, qb + h)),
                  pl.BlockSpec((BLK, LANES), lambda b, h: (b, kb + h)),
                  pl.BlockSpec((BLK, LANES), lambda b, h: (b, vb + h)),
                  pl.BlockSpec((4, HEAD_QK), lambda b, h: (0, 0)),
                  pl.BlockSpec((1, HEAD_V), lambda b, h: (0, 0))],
        out_specs=[pl.BlockSpec((BLK, LANES), lambda b, h: (b, h)),
                   pl.BlockSpec((1, 1, 1, 2, BLK, HEAD_QK), lambda b, h: (b, 0, h, 0, 0, 0)),
                   pl.BlockSpec((1, 1, 1, BLK, HEAD_V), lambda b, h: (b, 0, h, 0, 0))],
        out_shape=[jax.ShapeDtypeStruct((n * BLK, D_A), BF16),
                   jax.ShapeDtypeStruct((n, 1, H_A, 2, BLK, HEAD_QK), F32),
                   jax.ShapeDtypeStruct((n, 1, H_A, BLK, HEAD_V), F32)],
        compiler_params=_cparams("parallel", "parallel"),
        name="attn_prompt",
    )(proj, proj, proj, diff_lambda, subln_g.reshape(1, HEAD_V))


def _rope(x, cos, sin_signed):
    lane = lax.broadcasted_iota(jnp.int32, x.shape, 1)
    partner = jnp.where(lane % 32 < 16, pltpu.roll(x, LANES - 16, axis=1), pltpu.roll(x, 16, axis=1))
    return x * cos + partner * sin_signed


def _attn_sample_kernel(q_ref, k_ref, v_ref, ck_ref, cv_ref, cq_ref, sq_ref, ck_t_ref, sk_t_ref,
                        dl_ref, g_ref, o_ref, k_all, v_all, *, past, seq):
    @pl.when(pl.program_id(2) == 0)
    def _():
        scale = HEAD_QK ** -0.5
        k_all[0:past, :] = (jnp.concatenate([ck_ref[0, 0, 0, 0], ck_ref[0, 0, 0, 1]], axis=1)
                            * scale).astype(BF16)
        k_all[past:past + seq, :] = (_rope(k_ref[...], ck_t_ref[...], sk_t_ref[...]) * scale).astype(BF16)
        v_all[0:past, :] = cv_ref[0, 0, 0].astype(BF16)
        v_all[past:past + seq, :] = v_ref[...].astype(BF16)

    q = _rope(q_ref[...], cq_ref[...], sq_ref[...])
    o = _diff_attend(q, k_all[...], v_all[...], _lambda_value(dl_ref), g_ref[...])
    o_ref[...] = o.astype(o_ref.dtype)


def _attn_sample(lay, proj, cache_k, cache_v, cos_t, sin_t, diff_lambda, subln_g):
    qb, kb, vb = D_REST // LANES, (D_REST + D_A) // LANES, (D_REST + 2 * D_A) // LANES
    n, bps, pb = lay.n_sample, lay.bps, lay.pb
    seq = bps * BLK
    past = cache_k.shape[4]
    s0 = pb // bps
    assert pb % bps == 0
    return pl.pallas_call(
        functools.partial(_attn_sample_kernel, past=past, seq=seq),
        grid=(n, H_A, bps),
        in_specs=[pl.BlockSpec((BLK, LANES), lambda b, h, t: (pb + b * bps + t, qb + h)),
                  pl.BlockSpec((seq, LANES), lambda b, h, t: (s0 + b, kb + h)),
                  pl.BlockSpec((seq, LANES), lambda b, h, t: (s0 + b, vb + h)),
                  pl.BlockSpec((1, 1, 1, 2, past, HEAD_QK), lambda b, h, t: (b, 0, h, 0, 0, 0)),
                  pl.BlockSpec((1, 1, 1, past, HEAD_V), lambda b, h, t: (b, 0, h, 0, 0)),
                  pl.BlockSpec((BLK, LANES), lambda b, h, t: (t, 0)),
                  pl.BlockSpec((BLK, LANES), lambda b, h, t: (t, 0)),
                  pl.BlockSpec((seq, LANES), lambda b, h, t: (0, 0)),
                  pl.BlockSpec((seq, LANES), lambda b, h, t: (0, 0)),
                  pl.BlockSpec((4, HEAD_QK), lambda b, h, t: (0, 0)),
                  pl.BlockSpec((1, HEAD_V), lambda b, h, t: (0, 0))],
        out_specs=pl.BlockSpec((BLK, LANES), lambda b, h, t: (b * bps + t, h)),
        out_shape=jax.ShapeDtypeStruct((n * seq, D_A), BF16),
        scratch_shapes=[pltpu.VMEM((past + seq, LANES), BF16), pltpu.VMEM((past + seq, LANES), BF16)],
        compiler_params=_cparams("parallel", "parallel", "arbitrary"),
        name="attn_sample",
    )(proj, proj, proj, cache_k, cache_v, cos_t, sin_t, cos_t, sin_t, diff_lambda,
      subln_g.reshape(1, HEAD_V))


def _rope_tables(seq):
    t = jnp.arange(seq)
    lane = jnp.arange(LANES)
    n_freq = HEAD_QK // 4
    inv = ROPE_BASE ** (-(lane % n_freq).astype(F32) / n_freq)
    use_col = (lane % HEAD_QK) >= HEAD_QK // 2
    pos = jnp.where(use_col[None, :], (t % GRID_W)[:, None], (t // GRID_W)[:, None]).astype(F32)
    ang = pos * inv[None, :]
    sin_signed = jnp.where((lane % (2 * n_freq) < n_freq)[None, :], -jnp.sin(ang), jnp.sin(ang))
    return jnp.cos(ang), sin_signed


CONV_ROWS = 32
CONV_LANES = 512


CONV_SPAN = BLK + 2 * CONV_HALO - SUBLANES


def _conv_ln_kernel(u_ref, up_ref, un_ref, w_ref, b_ref, g_ref, beta_ref, o_ref, pad, tmp, shifted, *, lay):
    i = pl.program_id(0)
    first, last = lay.seq_pos(i)
    d = u_ref.shape[1]
    n_ct = d // CONV_LANES
    for ct in range(n_ct):
        sl = slice(ct * CONV_LANES, (ct + 1) * CONV_LANES)
        pad[ct, 0:CONV_HALO, :] = jnp.where(first, 0.0, up_ref[:, sl])
        pad[ct, CONV_HALO:CONV_HALO + BLK, :] = u_ref[:, sl]
        pad[ct, CONV_HALO + BLK:, :] = jnp.where(last, 0.0, un_ref[:, sl])
    off = CONV_HALO - CONV_PAD

    def conv_tile(ct, carry):
        for s in range(SUBLANES):
            shifted[s] = pad[ct, s:s + CONV_SPAN, :]
        for c in range(BLK // CONV_ROWS):
            r0 = c * CONV_ROWS
            acc = jnp.zeros((CONV_ROWS, CONV_LANES), F32) + b_ref[ct]
            for j in range(CONV_W):
                q, s = divmod(off + j, SUBLANES)
                a0 = r0 + q * SUBLANES
                acc = acc + shifted[s, a0:a0 + CONV_ROWS, :] * w_ref[ct, j:j + 1, :]
            tmp[ct, r0:r0 + CONV_ROWS, :] = acc
        return carry

    lax.fori_loop(0, n_ct, conv_tile, 0)

    def norm_rows(c, carry):
        r0 = pl.multiple_of(c * CONV_ROWS, CONV_ROWS)
        xs = [tmp[ct, pl.ds(r0, CONV_ROWS), :] for ct in range(n_ct)]
        mu = sum(jnp.sum(x, axis=-1, keepdims=True) for x in xs) * (1.0 / d)
        xs = [x - mu for x in xs]
        var = sum(jnp.sum(x * x, axis=-1, keepdims=True) for x in xs) * (1.0 / d)
        inv = lax.rsqrt(var + LN_EPS)
        for ct, x in enumerate(xs):
            sl = slice(ct * CONV_LANES, (ct + 1) * CONV_LANES)
            y = x * inv * g_ref[:, sl] + beta_ref[:, sl]
            o_ref[pl.ds(r0, CONV_ROWS), sl] = (y * jax.nn.sigmoid(y)).astype(o_ref.dtype)
        return carry

    lax.fori_loop(0, BLK // CONV_ROWS, norm_rows, 0)


def _conv_ln(lay, u, w_dw, b_dw, cln_g, cln_b):
    d = u.shape[1]
    hb = BLK // CONV_HALO
    n_halo = lay.rows // CONV_HALO
    n_ct = d // CONV_LANES
    vec = pl.BlockSpec((1, d), lambda i: (0, 0))
    w_t = w_dw.reshape(CONV_W, n_ct, CONV_LANES).transpose(1, 0, 2)
    return pl.pallas_call(
        functools.partial(_conv_ln_kernel, lay=lay),
        grid=(lay.nb,),
        in_specs=[pl.BlockSpec((BLK, d), lambda i: (i, 0)),
                  pl.BlockSpec((CONV_HALO, d), lambda i: (jnp.maximum(i * hb - 1, 0), 0)),
                  pl.BlockSpec((CONV_HALO, d), lambda i: (jnp.minimum((i + 1) * hb, n_halo - 1), 0)),
                  pl.BlockSpec((n_ct, CONV_W, CONV_LANES), lambda i: (0, 0, 0)),
                  pl.BlockSpec((n_ct, 1, CONV_LANES), lambda i: (0, 0, 0)), vec, vec],
        out_specs=pl.BlockSpec((BLK, d), lambda i: (i, 0)),
        out_shape=jax.ShapeDtypeStruct(u.shape, BF16),
        scratch_shapes=[pltpu.VMEM((n_ct, BLK + 2 * CONV_HALO, CONV_LANES), F32),
                        pltpu.VMEM((n_ct, BLK, CONV_LANES), F32),
                        pltpu.VMEM((SUBLANES, CONV_SPAN, CONV_LANES), F32)],
        compiler_params=_cparams("parallel"),
        name="conv_ln",
    )(u, u, u, w_t, b_dw.reshape(n_ct, 1, CONV_LANES), cln_g.reshape(1, d), cln_b.reshape(1, d))


def _pad_cols(x, n):
    return jnp.pad(x, ((0, 0), (0, n - x.shape[1])))


def _rest_layout(x):
    c = 3 * D_B
    return jnp.concatenate([x[:, :c], _pad_cols(x[:, c:c + LORA_W], LANES),
                            _pad_cols(x[:, c + LORA_W:c + LORA_W + LORA_A], LANES),
                            x[:, c + LORA_W + LORA_A:]], axis=1)


def _ffn_weights(w_up, w_dw, w_down):
    padc = lambda x: jnp.concatenate([_pad_cols(x[:, :D_FF], D_FF_PAD), _pad_cols(x[:, D_FF:], D_FF_PAD)], axis=1)
    return (padc(w_up).astype(BF16), padc(w_dw),
            jnp.pad(w_down, ((0, D_FF_PAD - D_FF), (0, 0))).astype(BF16))


def _pair_state(s):
    b = s.shape[0]
    return s.reshape(b, N_PAIR, 2, HEAD_B, HEAD_B).transpose(0, 1, 3, 2, 4).reshape(b, N_PAIR, HEAD_B, LANES)


def _unpair_state(s):
    b = s.shape[0]
    return s.reshape(b, N_PAIR, HEAD_B, 2, HEAD_B).transpose(0, 1, 3, 2, 4).reshape(b, H_B, HEAD_B, HEAD_B)


def kernel(x_prompt, x_sample, cache_k, cache_v, state_wkv_fwd, state_wkv_bwd, c, c_ctx, w_ada, b_ada, norm_g, final_norm_g, w_in, w_out, diff_lambda, subln_g, shift_mu, w0, w2, a0, a2, g2, k_k, k_a, r_k, lnx_g, lnx_b, w_pw1, w_dw, b_dw, cln_g, cln_b, w_pw2, w_up, w_ffn_dw, w_down):
    n_prompt, seq_p, d = x_prompt.shape
    n_sample, seq_s, _ = x_sample.shape
    assert seq_p == BLK and seq_s % BLK == 0 and d == D_MODEL
    lay = _Layout(n_prompt, n_sample, seq_s)
    tm = lay.tile_rows()
    tm_small = min(tm, 512)

    x = jnp.concatenate([x_prompt.reshape(-1, d), x_sample.reshape(-1, d)], axis=0)
    c16 = jnp.concatenate([c_ctx[None, :], c, jnp.zeros((N_COND - 1 - n_sample, d), F32)], axis=0)
    mod = _adaln(c16, w_ada, b_ada)
    mod = mod.reshape(mod.shape[0], N_COND, 1, 6 * d)
    e3 = _seg_ones()

    e = 0
    w_rest = w_in[e][:, 2 * D_A + D_A:]
    w_in_p = jnp.concatenate([_rest_layout(w_rest), w_in[e][:, :3 * D_A]], axis=1).astype(BF16)
    mu_p = _rest_layout(shift_mu[e])
    pad_rows = lambda w: jnp.pad(w, ((0, 0), (0, LANES - w.shape[1]), (0, 0))).astype(BF16)

    h = _norm_mod(lay, x, norm_g[0, 0], mod[0], 0)
    proj = _mm(h, w_in_p, tm, 512)
    oa_p, new_k, new_v = _attn_prompt(lay, proj, diff_lambda[e], subln_g[e])
    cos_t, sin_t = _rope_tables(seq_s)
    oa_s = _attn_sample(lay, proj, cache_k[:, e:e + 1], cache_v[:, e:e + 1], cos_t, sin_t,
                        diff_lambda[e], subln_g[e])
    o_a = jnp.concatenate([oa_p, oa_s], axis=0)

    r_, v_, a_, wf, kf, bf, wb, kb, bb, g_, bonus = _rwkv_prep(
        lay, proj, mu_p, w0[e], pad_rows(w2[e]), a0[e], pad_rows(a2[e]), g2[e].astype(BF16),
        k_k[e].reshape(1, D_B), k_a[e].reshape(1, D_B), r_k[e].reshape(1, D_B), e3)
    init = jnp.stack([_pair_state(state_wkv_fwd[:, e]), _pair_state(state_wkv_bwd[:, e])], axis=1)
    yf, yb, fin = _rwkv_scan(lay, r_, v_, a_, wf, kf, bf, wb, kb, bb, init, _seg_ones_pair())
    o_b = _rwkv_post(lay, yf, yb, bonus, g_, lnx_g[e].reshape(1, D_B), lnx_b[e].reshape(1, D_B), e3)
    new_sf = _unpair_state(fin[:n_prompt, 0])[:, None]
    new_sb = _unpair_state(fin[:n_prompt, 1])[:, None]

    w_out_b = w_out[e].astype(BF16)
    x = _mm_res(lay, [o_a, o_b], [w_out_b[:D_A], w_out_b[D_A:]], x, mod[0], 2, tm_small, 512)
    wu, wd, wdn = _ffn_weights(w_up[0], w_ffn_dw[0], w_down[0])
    h = _norm_mod(lay, x, norm_g[0, 1], mod[0], 3)
    u = _ffn_up(lay, h, wu, wd, tm, 512)
    x = _mm_res(lay, [u], [wdn], x, mod[0], 5, tm_small, 512)

    o = 0
    h = _norm_mod(lay, x, norm_g[1, 0], mod[1], 0)
    u = _mm_glu(h, w_pw1[o].astype(BF16), tm, 512)
    u = _conv_ln(lay, u, w_dw[o], b_dw[o], cln_g[o], cln_b[o])
    x = _mm_res(lay, [u], [w_pw2[o].astype(BF16)], x, mod[1], 2, tm_small, 512)
    wu, wd, wdn = _ffn_weights(w_up[1], w_ffn_dw[1], w_down[1])
    h = _norm_mod(lay, x, norm_g[1, 1], mod[1], 3)
    u = _ffn_up(lay, h, wu, wd, tm, 512)
    x = _mm_res(lay, [u], [wdn], x, mod[1], 5, tm_small, 512)

    y_p = _final_norm(x, final_norm_g, 0, lay.pb).reshape(n_prompt, seq_p, d)
    y_s = _final_norm(x, final_norm_g, lay.pb, lay.nb - lay.pb).reshape(n_sample, seq_s, d)
    return (y_p, y_s, new_k, new_v, new_sf, new_sb)
```

```python
import functools
import math

import jax
import jax.numpy as jnp
from jax import lax
from jax.experimental import pallas as pl
from jax.experimental.pallas import tpu as pltpu

F32 = jnp.float32
BF16 = jnp.bfloat16

LANES = 128
SUBLANES = 8
VMEM_LIMIT_BYTES = 56 * 1024 * 1024

D_MODEL = 2048
BLK = 256
GRID_W = 64
HEAD_QK = 64
HEAD_V = 128
H_A = 8
D_A = 1024
HEAD_B = 64
H_B = 16
D_B = 1024
N_PAIR = D_B // LANES
LORA_W = 96
LORA_A = 96
LORA_G = 256
D_REST = 3 * D_B + 2 * LANES + LORA_G
D_PROJ = D_REST + 3 * D_A
D_FF = 5504
CONV_W = 31
CONV_PAD = (CONV_W - 1) // 2
CONV_HALO = 16
ATTN_Q_ROWS = 512
RMS_EPS = 1e-6
LN_EPS = 1e-5
GN_EPS = 64e-5
ROPE_BASE = 10000.0
LAM_INIT0 = 0.8 - 0.6 * math.exp(-0.3 * 0)
N_SPLIT = 2
N_COND = 16


def _cparams(*sem):
    return pltpu.CompilerParams(dimension_semantics=sem, vmem_limit_bytes=VMEM_LIMIT_BYTES)


class _Layout:
    def __init__(self, n_prompt, n_sample, sample_len):
        self.n_prompt = n_prompt
        self.n_sample = n_sample
        self.bps = sample_len // BLK
        self.pb = n_prompt
        self.nb = n_prompt + n_sample * self.bps
        self.rows = self.nb * BLK
        assert 1 + n_sample <= N_COND

    def cond_of_block(self, i):
        return jnp.where(i < self.pb, 0, 1 + (i - self.pb) // self.bps)

    def seq_pos(self, i):
        c = (i - self.pb) % self.bps
        first = jnp.logical_or(i < self.pb, c == 0)
        last = jnp.logical_or(i < self.pb, c == self.bps - 1)
        return first, last

    def tile_rows(self):
        t = self.bps * BLK
        while (self.pb * BLK) % t:
            t //= 2
        return t


def _adaln_kernel(c_ref, w_ref, b_ref, o_ref):
    x = c_ref[...]
    s = (x * jax.nn.sigmoid(x)).astype(BF16)
    o_ref[0] = jnp.dot(s, w_ref[0].astype(BF16), preferred_element_type=F32) + b_ref[0]


def _adaln(c16, w_ada, b_ada):
    depth, d, n = w_ada.shape
    tn = 1024
    return pl.pallas_call(
        _adaln_kernel,
        grid=(depth, n // tn),
        in_specs=[pl.BlockSpec((N_COND, d), lambda l, j: (0, 0)),
                  pl.BlockSpec((1, d, tn), lambda l, j: (l, 0, j)),
                  pl.BlockSpec((1, 1, tn), lambda l, j: (l, 0, j))],
        out_specs=pl.BlockSpec((1, N_COND, tn), lambda l, j: (l, 0, j)),
        out_shape=jax.ShapeDtypeStruct((depth, N_COND, n), F32),
        compiler_params=_cparams("parallel", "parallel"),
        name="adaln",
    )(c16, w_ada, b_ada.reshape(depth, 1, n))


def _norm_mod_kernel(x_ref, g_ref, sh_ref, sc_ref, o_ref):
    x = x_ref[...]
    y = x * lax.rsqrt(jnp.mean(x * x, axis=-1, keepdims=True) + RMS_EPS) * g_ref[...]
    o_ref[...] = (y * (1.0 + sc_ref[0]) + sh_ref[0]).astype(o_ref.dtype)


def _norm_mod(lay, x, g, mod, shift_idx):
    d = x.shape[1]
    cond = lay.cond_of_block
    return pl.pallas_call(
        _norm_mod_kernel,
        grid=(lay.nb,),
        in_specs=[pl.BlockSpec((BLK, d), lambda i: (i, 0)),
                  pl.BlockSpec((1, d), lambda i: (0, 0)),
                  pl.BlockSpec((1, 1, d), lambda i: (cond(i), 0, shift_idx)),
                  pl.BlockSpec((1, 1, d), lambda i: (cond(i), 0, shift_idx + 1))],
        out_specs=pl.BlockSpec((BLK, d), lambda i: (i, 0)),
        out_shape=jax.ShapeDtypeStruct(x.shape, BF16),
        compiler_params=_cparams("parallel"),
        name="norm_mod",
    )(x, g.reshape(1, d), mod, mod)


def _final_norm_kernel(x_ref, g_ref, o_ref):
    x = x_ref[...]
    o_ref[...] = x * lax.rsqrt(jnp.mean(x * x, axis=-1, keepdims=True) + RMS_EPS) * g_ref[...]


def _final_norm(x, g, blk0, nblk):
    d = x.shape[1]
    return pl.pallas_call(
        _final_norm_kernel,
        grid=(nblk,),
        in_specs=[pl.BlockSpec((BLK, d), lambda i: (i + blk0, 0)),
                  pl.BlockSpec((1, d), lambda i: (0, 0))],
        out_specs=pl.BlockSpec((BLK, d), lambda i: (i, 0)),
        out_shape=jax.ShapeDtypeStruct((nblk * BLK, d), F32),
        compiler_params=_cparams("parallel"),
        name="final_norm",
    )(x, g.reshape(1, d))


def _mm_kernel(a_ref, w_ref, o_ref):
    o_ref[...] = jnp.dot(a_ref[...], w_ref[...], preferred_element_type=F32)


def _mm(a, w, tm, tn):
    m, k = a.shape
    n = w.shape[1]
    return pl.pallas_call(
        _mm_kernel,
        grid=(m // tm, n // tn),
        in_specs=[pl.BlockSpec((tm, k), lambda i, j: (i, 0)),
                  pl.BlockSpec((k, tn), lambda i, j: (0, j))],
        out_specs=pl.BlockSpec((tm, tn), lambda i, j: (i, j)),
        out_shape=jax.ShapeDtypeStruct((m, n), F32),
        compiler_params=_cparams("parallel", "parallel"),
        name="mm",
    )(a, w)


def _mm_res_kernel(*refs, n_in):
    a_refs, w_refs = refs[:n_in], refs[n_in:2 * n_in]
    x_ref, g_ref, o_ref = refs[2 * n_in:]
    acc = jnp.dot(a_refs[0][...], w_refs[0][...], preferred_element_type=F32)
    for a_ref, w_ref in zip(a_refs[1:], w_refs[1:]):
        acc += jnp.dot(a_ref[...], w_ref[...], preferred_element_type=F32)
    o_ref[...] = x_ref[...] + g_ref[0] * acc


def _mm_res(lay, a_list, w, x, mod, gate_idx, tm, tn):
    m, n = x.shape
    bpt = tm // BLK
    cond = lambda i: lay.cond_of_block(i * bpt)
    ka = a_list[0].shape[1]
    assert all(a.shape[1] == ka for a in a_list) and w.shape[0] == ka * len(a_list)
    in_specs = ([pl.BlockSpec((tm, ka), lambda i, j: (i, 0)) for _ in a_list]
                + [pl.BlockSpec((ka, tn), functools.partial(lambda i, j, r: (r, j), r=r))
                   for r in range(len(a_list))]
                + [pl.BlockSpec((tm, tn), lambda i, j: (i, j)),
                   pl.BlockSpec((1, 1, tn), lambda i, j: (cond(i), 0, gate_idx * (n // tn) + j))])
    return pl.pallas_call(
        functools.partial(_mm_res_kernel, n_in=len(a_list)),
        grid=(m // tm, n // tn),
        in_specs=in_specs,
        out_specs=pl.BlockSpec((tm, tn), lambda i, j: (i, j)),
        out_shape=jax.ShapeDtypeStruct((m, n), F32),
        compiler_params=_cparams("parallel", "parallel"),
        name="mm_res",
    )(*a_list, *([w] * len(a_list)), x, mod)


def _mm_glu_kernel(a_ref, wa_ref, wb_ref, o_ref):
    a = a_ref[...]
    u = jnp.dot(a, wa_ref[...], preferred_element_type=F32)
    v = jnp.dot(a, wb_ref[...], preferred_element_type=F32)
    o_ref[...] = u * jax.nn.sigmoid(v)


def _mm_glu(a, w, tm, tn):
    m, k = a.shape
    n = w.shape[1] // 2
    nj = n // tn
    return pl.pallas_call(
        _mm_glu_kernel,
        grid=(m // tm, nj),
        in_specs=[pl.BlockSpec((tm, k), lambda i, j: (i, 0)),
                  pl.BlockSpec((k, tn), lambda i, j: (0, j)),
                  pl.BlockSpec((k, tn), lambda i, j: (0, j + nj))],
        out_specs=pl.BlockSpec((tm, tn), lambda i, j: (i, j)),
        out_shape=jax.ShapeDtypeStruct((m, n), F32),
        compiler_params=_cparams("parallel", "parallel"),
        name="mm_glu",
    )(a, w, w)


def _row_index(shape):
    return lax.broadcasted_iota(jnp.int32, shape, 0)


def _ffn_up_kernel(a_ref, wg_ref, wv_ref, dg_ref, dv_ref, o_ref, *, lay, tm):
    i = pl.program_id(0)
    seq_len = jnp.where(i * (tm // BLK) < lay.pb, BLK, lay.bps * BLK)
    a = a_ref[...]

    def conv(w_ref, d_ref):
        u = jnp.dot(a, w_ref[...], preferred_element_type=F32)
        pos = _row_index(u.shape) & (seq_len - 1)
        prev = jnp.where(pos == 0, 0.0, pltpu.roll(u, 1, axis=0))
        nxt = jnp.where(pos == seq_len - 1, 0.0, pltpu.roll(u, tm - 1, axis=0))
        return prev * d_ref[0:1, :] + u * d_ref[1:2, :] + nxt * d_ref[2:3, :]

    gate = conv(wg_ref, dg_ref)
    val = conv(wv_ref, dv_ref)
    o_ref[...] = (gate * jax.nn.sigmoid(gate) * val).astype(o_ref.dtype)


def _ffn_up(lay, a, w_gate, w_val, dw_gate, dw_val, tm, tn):
    m, k = a.shape
    f = w_gate.shape[1]
    w_spec = pl.BlockSpec((k, tn), lambda i, j: (0, j))
    dw_spec = pl.BlockSpec((3, tn), lambda i, j: (0, j))
    return pl.pallas_call(
        functools.partial(_ffn_up_kernel, lay=lay, tm=tm),
        grid=(m // tm, pl.cdiv(f, tn)),
        in_specs=[pl.BlockSpec((tm, k), lambda i, j: (i, 0)), w_spec, w_spec, dw_spec, dw_spec],
        out_specs=pl.BlockSpec((tm, tn), lambda i, j: (i, j)),
        out_shape=jax.ShapeDtypeStruct((m, f), BF16),
        compiler_params=_cparams("parallel", "parallel"),
        name="ffn_up",
    )(a, w_gate, w_val, dw_gate, dw_val)


def _split_bf16(x):
    parts = []
    r = x
    for s in range(N_SPLIT):
        p = r.astype(BF16)
        parts.append(p)
        if s + 1 < N_SPLIT:
            r = r - p.astype(F32)
    return parts


def _seg_sum(x, e_ref):
    lhs = jnp.concatenate(_split_bf16(x), axis=1)
    return jnp.dot(lhs, e_ref[...], preferred_element_type=F32)


def _seg_ones():
    r = jnp.arange(N_SPLIT * LANES)[:, None] % LANES
    c = jnp.arange(LANES)[None, :]
    return (r // HEAD_B == c // HEAD_B).astype(BF16)


def _seg_ones_pair():
    r = jnp.arange(2 * LANES)[:, None]
    c = jnp.arange(2 * LANES)[None, :]
    return (r // HEAD_B == c // HEAD_B).astype(BF16)


def _softplus(z):
    return jnp.maximum(z, 0.0) + jnp.log(1.0 + jnp.exp(-jnp.abs(z)))


def _prep_kernel(p_ref, pp_ref, pn_ref, mu_ref, w0_ref, w2_ref, a0_ref, a2_ref, g2_ref,
                 kk_ref, ka_ref, rk_ref, e_ref,
                 r_o, v_o, a_o, wf_o, kf_o, bf_o, wb_o, kb_o, bb_o, g_o, bonus_o, *, lay):
    i = pl.program_id(0)
    first, last = lay.seq_pos(i)
    rows = _row_index((BLK, 1))

    def shifted(c0, c1):
        p = p_ref[:, c0:c1]
        prev_row = jnp.where(first, 0.0, pp_ref[SUBLANES - 1:SUBLANES, c0:c1])
        next_row = jnp.where(last, 0.0, pn_ref[0:1, c0:c1])
        p_prev = jnp.where(rows == 0, prev_row, pltpu.roll(p, 1, axis=0))
        p_next = jnp.where(rows == BLK - 1, next_row, pltpu.roll(p, BLK - 1, axis=0))
        return p + mu_ref[0:1, c0:c1] * (p_prev - p) + mu_ref[1:2, c0:c1] * (p_next - p)

    c_w = 3 * D_B
    c_a = c_w + LANES
    c_g = c_a + LANES
    tw = jnp.tanh(shifted(c_w, c_a)).astype(BF16)
    xa = shifted(c_a, c_g).astype(BF16)
    sg = jax.nn.sigmoid(shifted(c_g, D_REST)).astype(BF16)

    for j in range(N_PAIR):
        sl = slice(j * LANES, (j + 1) * LANES)
        r = shifted(j * LANES, (j + 1) * LANES)
        kb = shifted(D_B + j * LANES, D_B + (j + 1) * LANES)
        vb = shifted(2 * D_B + j * LANES, 2 * D_B + (j + 1) * LANES)
        r_o[:, sl] = r
        v_o[:, sl] = vb
        g_o[:, sl] = jnp.dot(sg, g2_ref[:, sl], preferred_element_type=F32)
        kk = kb * kk_ref[:, sl]
        nrm = jnp.sqrt(_seg_sum(kk * kk, e_ref))
        kk = kk / jnp.maximum(nrm, 1e-12)
        a_o[:, sl] = -kk
        bonus_o[:, sl] = _seg_sum(r * kb * rk_ref[:, sl], e_ref) * vb
        for d, (w_o, k_o, b_o) in enumerate(((wf_o, kf_o, bf_o), (wb_o, kb_o, bb_o))):
            lw = w0_ref[d:d + 1, sl] + jnp.dot(tw, w2_ref[d, :, sl], preferred_element_type=F32)
            w_log = -_softplus(-lw) - 0.5
            w_o[:, sl] = jnp.exp(-jnp.exp(w_log))
            a_lr = jax.nn.sigmoid(a0_ref[d:d + 1, sl]
                                  + jnp.dot(xa, a2_ref[d, :, sl], preferred_element_type=F32))
            k_o[:, sl] = kb * (1.0 + (a_lr - 1.0) * ka_ref[:, sl])
            b_o[:, sl] = kk * a_lr


def _rwkv_prep(lay, proj, mu, w0, w2p, a0, a2p, g2, k_k, k_a, r_k, e3):
    nb = lay.nb
    hb = BLK // SUBLANES
    n_halo = lay.rows // SUBLANES
    full = lambda shape: pl.BlockSpec(shape, lambda i: (0,) * len(shape))
    out = jax.ShapeDtypeStruct((lay.rows, D_B), F32)
    return pl.pallas_call(
        functools.partial(_prep_kernel, lay=lay),
        grid=(nb,),
        in_specs=[pl.BlockSpec((BLK, D_REST), lambda i: (i, 0)),
                  pl.BlockSpec((SUBLANES, D_REST), lambda i: (jnp.maximum(i * hb - 1, 0), 0)),
                  pl.BlockSpec((SUBLANES, D_REST), lambda i: (jnp.minimum((i + 1) * hb, n_halo - 1), 0)),
                  full((2, D_REST)), full((2, D_B)), full((2, LANES, D_B)), full((2, D_B)),
                  full((2, LANES, D_B)), full((LORA_G, D_B)), full((1, D_B)), full((1, D_B)),
                  full((1, D_B)), full((N_SPLIT * LANES, LANES))],
        out_specs=[pl.BlockSpec((BLK, D_B), lambda i: (i, 0))] * 11,
        out_shape=[out] * 11,
        compiler_params=_cparams("parallel"),
        name="rwkv_prep",
    )(proj, proj, proj, mu, w0, w2p, a0, a2p, g2, k_k, k_a, r_k, e3)


def _scan_kernel(rf, vf, af, wf, kf, bf, rb, vb, ab, wb, kb, bb, init_ref, e_ref,
                 yf_ref, yb_ref, fin_ref, st, ybuf, *, lay):
    i = pl.program_id(0)
    first, _ = lay.seq_pos(i)

    @pl.when(first)
    def _():
        st[...] = jnp.where(i >= lay.pb, init_ref[0], 0.0)

    diag = (lax.broadcasted_iota(jnp.int32, (HEAD_B, LANES), 0)
            == lax.broadcasted_iota(jnp.int32, (HEAD_B, LANES), 1) % HEAD_B)
    dirs = ((rf, vf, af, wf, kf, bf, yf_ref), (rb, vb, ab, wb, kb, bb, yb_ref))

    def seg_sums(tiles):
        bt = [t.astype(BF16) for t in tiles]
        lhs = jnp.concatenate([jnp.concatenate(bt[k:k + 2], axis=1) for k in range(0, len(bt), 2)], axis=0)
        res = jnp.dot(lhs, e_ref[...], preferred_element_type=F32)
        return [res[(k // 2) * HEAD_B:(k // 2 + 1) * HEAD_B, (k % 2) * LANES:(k % 2 + 1) * LANES]
                for k in range(len(bt))]

    def group(gi, carry):
        bases = (pl.multiple_of(gi * SUBLANES, SUBLANES),
                 pl.multiple_of(BLK - SUBLANES - gi * SUBLANES, SUBLANES))

        def row(ref, d, j, sl):
            return ref[pl.ds(bases[d], SUBLANES), sl][j:j + 1, :]

        def emit_y(js, tiles):
            for d in range(2):
                for p in range(N_PAIR):
                    ybuf[d, js[d]:js[d] + 1, p * LANES:(p + 1) * LANES] = jnp.sum(
                        jnp.where(diag, tiles[d * N_PAIR + p], 0.0), axis=0, keepdims=True)

        js_prev = None
        for t in range(SUBLANES):
            js = (t, SUBLANES - 1 - t)
            xs, ys = [], []
            for d in range(2):
                r_r, v_r, a_r = dirs[d][0], dirs[d][1], dirs[d][2]
                for p in range(N_PAIR):
                    sl = slice(p * LANES, (p + 1) * LANES)
                    s_cur = st[d, p]
                    xs.append(s_cur * row(a_r, d, js[d], sl))
                    xs.append(jnp.where(diag, row(v_r, d, js[d], sl), 0.0))
                    if js_prev is not None:
                        ys.append(s_cur * row(r_r, d, js_prev[d], sl))
            res = seg_sums(xs + ys)
            if js_prev is not None:
                emit_y(js_prev, res[len(xs):])
            for d in range(2):
                w_r, k_r, b_r = dirs[d][3], dirs[d][4], dirs[d][5]
                for p in range(N_PAIR):
                    sl = slice(p * LANES, (p + 1) * LANES)
                    idx = d * N_PAIR + p
                    sa, vcol = res[2 * idx], res[2 * idx + 1]
                    st[d, p] = (st[d, p] * row(w_r, d, js[d], sl) + sa * row(b_r, d, js[d], sl)
                                + vcol * row(k_r, d, js[d], sl))
            js_prev = js
        emit_y(js_prev, seg_sums([st[d, p] * row(dirs[d][0], d, js_prev[d], slice(p * LANES, (p + 1) * LANES))
                                  for d in range(2) for p in range(N_PAIR)]))
        for d in range(2):
            dirs[d][6][pl.ds(bases[d], SUBLANES), :] = ybuf[d]
        return carry

    lax.fori_loop(0, BLK // SUBLANES, group, 0)
    fin_ref[0] = st[...]


def _rwkv_scan(lay, r, v, a, wf, kf, bf, wb, kb, bb, init, e3):
    pb, bps = lay.pb, lay.bps

    def bwd_block(i):
        c = (i - pb) % bps
        return jnp.where(i < pb, i, i - c + (bps - 1 - c))

    fwd = pl.BlockSpec((BLK, D_B), lambda i: (i, 0))
    bwd = pl.BlockSpec((BLK, D_B), lambda i: (bwd_block(i), 0))
    st_shape = (2, N_PAIR, HEAD_B, LANES)
    y = jax.ShapeDtypeStruct((lay.rows, D_B), F32)
    return pl.pallas_call(
        functools.partial(_scan_kernel, lay=lay),
        grid=(lay.nb,),
        in_specs=[fwd] * 6 + [bwd] * 6
        + [pl.BlockSpec((1,) + st_shape, lambda i: (jnp.maximum(i - pb, 0) // bps, 0, 0, 0, 0)),
           pl.BlockSpec((2 * LANES, 2 * LANES), lambda i: (0, 0))],
        out_specs=[fwd, bwd, pl.BlockSpec((1,) + st_shape, lambda i: (i, 0, 0, 0, 0))],
        out_shape=[y, y, jax.ShapeDtypeStruct((lay.nb,) + st_shape, F32)],
        scratch_shapes=[pltpu.VMEM(st_shape, F32), pltpu.VMEM((2, SUBLANES, D_B), F32)],
        compiler_params=_cparams("arbitrary"),
        name="rwkv_scan",
    )(r, v, a, wf, kf, bf, r, v, a, wb, kb, bb, init, e3)


def _post_kernel(yf_ref, yb_ref, bonus_ref, g_ref, lg_ref, lb_ref, e_ref, o_ref):
    for j in range(N_PAIR):
        sl = slice(j * LANES, (j + 1) * LANES)
        y = yf_ref[:, sl] + yb_ref[:, sl]
        mu = _seg_sum(y, e_ref) * (1.0 / HEAD_B)
        dlt = y - mu
        var = _seg_sum(dlt * dlt, e_ref) * (1.0 / HEAD_B)
        yn = dlt * lax.rsqrt(var + GN_EPS) * lg_ref[:, sl] + lb_ref[:, sl]
        o_ref[:, sl] = ((yn + bonus_ref[:, sl]) * g_ref[:, sl]).astype(o_ref.dtype)


def _rwkv_post(lay, yf, yb, bonus, g, lnx_g, lnx_b, e3):
    blk = pl.BlockSpec((BLK, D_B), lambda i: (i, 0))
    vec = pl.BlockSpec((1, D_B), lambda i: (0, 0))
    return pl.pallas_call(
        _post_kernel,
        grid=(lay.nb,),
        in_specs=[blk, blk, blk, blk, vec, vec, pl.BlockSpec((N_SPLIT * LANES, LANES), lambda i: (0, 0))],
        out_specs=blk,
        out_shape=jax.ShapeDtypeStruct((lay.rows, D_B), BF16),
        compiler_params=_cparams("parallel"),
        name="rwkv_post",
    )(yf, yb, bonus, g, lnx_g, lnx_b, e3)


def _lambda_value(dl_ref):
    dl = dl_ref[...]
    s01 = jnp.sum(dl[0:1] * dl[1:2], axis=-1, keepdims=True)
    s23 = jnp.sum(dl[2:3] * dl[3:4], axis=-1, keepdims=True)
    return jnp.exp(s01) - jnp.exp(s23) + LAM_INIT0


def _softmax_rows(s):
    e = jnp.exp(s - jnp.max(s, axis=-1, keepdims=True))
    return e / jnp.sum(e, axis=-1, keepdims=True)


def _diff_attend(q, k_bf, v_bf, lam, g_row):
    lane = lax.broadcasted_iota(jnp.int32, q.shape, 1)
    nt = (((1,), (1,)), ((), ()))
    q1 = jnp.where(lane < HEAD_QK, q, 0.0).astype(BF16)
    q2 = jnp.where(lane >= HEAD_QK, q, 0.0).astype(BF16)
    p1 = _softmax_rows(lax.dot_general(q1, k_bf, nt, preferred_element_type=F32))
    p2 = _softmax_rows(lax.dot_general(q2, k_bf, nt, preferred_element_type=F32))
    a = (p1 - lam * p2).astype(BF16)
    o = jnp.dot(a, v_bf, preferred_element_type=F32)
    o = o * lax.rsqrt(jnp.mean(o * o, axis=-1, keepdims=True) + LN_EPS) * g_row
    return o * (1.0 - LAM_INIT0)


def _attn_prompt_kernel(q_ref, k_ref, v_ref, dl_ref, g_ref, o_ref, ck_ref, cv_ref):
    k = k_ref[...]
    v = v_ref[...]
    ck_ref[0, 0, 0, 0] = k[:, :HEAD_QK]
    ck_ref[0, 0, 0, 1] = k[:, HEAD_QK:]
    cv_ref[0, 0, 0] = v
    o = _diff_attend(q_ref[...], (k * HEAD_QK ** -0.5).astype(BF16), v.astype(BF16),
                     _lambda_value(dl_ref), g_ref[...])
    o_ref[...] = o.astype(o_ref.dtype)


def _attn_prompt(lay, proj, diff_lambda, subln_g):
    qb, kb, vb = D_REST // LANES, (D_REST + D_A) // LANES, (D_REST + 2 * D_A) // LANES
    n = lay.n_prompt
    return pl.pallas_call(
        _attn_prompt_kernel,
        grid=(n, H_A),
        in_specs=[pl.BlockSpec((BLK, LANES), lambda b, h: (b, qb + h)),
                  pl.BlockSpec((BLK, LANES), lambda b, h: (b, kb + h)),
                  pl.BlockSpec((BLK, LANES), lambda b, h: (b, vb + h)),
                  pl.BlockSpec((4, HEAD_QK), lambda b, h: (0, 0)),
                  pl.BlockSpec((1, HEAD_V), lambda b, h: (0, 0))],
        out_specs=[pl.BlockSpec((BLK, LANES), lambda b, h: (b, h)),
                   pl.BlockSpec((1, 1, 1, 2, BLK, HEAD_QK), lambda b, h: (b, 0, h, 0, 0, 0)),
                   pl.BlockSpec((1, 1, 1, BLK, HEAD_V), lambda b, h: (b, 0, h, 0, 0))],
        out_shape=[jax.ShapeDtypeStruct((n * BLK, D_A), BF16),
                   jax.ShapeDtypeStruct((n, 1, H_A, 2, BLK, HEAD_QK), F32),
                   jax.ShapeDtypeStruct((n, 1, H_A, BLK, HEAD_V), F32)],
        compiler_params=_cparams("parallel", "parallel"),
        name="attn_prompt",
    )(proj, proj, proj, diff_lambda, subln_g.reshape(1, HEAD_V))


def _rope(x, cos, sin_signed):
    lane = lax.broadcasted_iota(jnp.int32, x.shape, 1)
    partner = jnp.where(lane % 32 < 16, pltpu.roll(x, LANES - 16, axis=1), pltpu.roll(x, 16, axis=1))
    return x * cos + partner * sin_signed


def _attn_sample_kernel(q_ref, k_ref, v_ref, ck_ref, cv_ref, cq_ref, sq_ref, ck_t_ref, sk_t_ref,
                        dl_ref, g_ref, o_ref, k_all, v_all, *, past, seq):
    @pl.when(pl.program_id(2) == 0)
    def _():
        scale = HEAD_QK ** -0.5
        k_all[0:past, :] = (jnp.concatenate([ck_ref[0, 0, 0, 0], ck_ref[0, 0, 0, 1]], axis=1)
                            * scale).astype(BF16)
        k_all[past:past + seq, :] = (_rope(k_ref[...], ck_t_ref[...], sk_t_ref[...]) * scale).astype(BF16)
        v_all[0:past, :] = cv_ref[0, 0, 0].astype(BF16)
        v_all[past:past + seq, :] = v_ref[...].astype(BF16)

    q = _rope(q_ref[...], cq_ref[...], sq_ref[...])
    o = _diff_attend(q, k_all[...], v_all[...], _lambda_value(dl_ref), g_ref[...])
    o_ref[...] = o.astype(o_ref.dtype)


def _attn_sample(lay, proj, cache_k, cache_v, cos_t, sin_t, diff_lambda, subln_g):
    qb, kb, vb = D_REST // LANES, (D_REST + D_A) // LANES, (D_REST + 2 * D_A) // LANES
    n, bps, pb = lay.n_sample, lay.bps, lay.pb
    seq = bps * BLK
    past = cache_k.shape[4]
    s0 = pb // bps
    assert pb % bps == 0
    tq = ATTN_Q_ROWS if seq % ATTN_Q_ROWS == 0 else BLK
    nq = seq // tq
    q0 = pb * BLK // tq
    return pl.pallas_call(
        functools.partial(_attn_sample_kernel, past=past, seq=seq),
        grid=(n, H_A, nq),
        in_specs=[pl.BlockSpec((tq, LANES), lambda b, h, t: (q0 + b * nq + t, qb + h)),
                  pl.BlockSpec((seq, LANES), lambda b, h, t: (s0 + b, kb + h)),
                  pl.BlockSpec((seq, LANES), lambda b, h, t: (s0 + b, vb + h)),
                  pl.BlockSpec((1, 1, 1, 2, past, HEAD_QK), lambda b, h, t: (b, 0, h, 0, 0, 0)),
                  pl.BlockSpec((1, 1, 1, past, HEAD_V), lambda b, h, t: (b, 0, h, 0, 0)),
                  pl.BlockSpec((tq, LANES), lambda b, h, t: (t, 0)),
                  pl.BlockSpec((tq, LANES), lambda b, h, t: (t, 0)),
                  pl.BlockSpec((seq, LANES), lambda b, h, t: (0, 0)),
                  pl.BlockSpec((seq, LANES), lambda b, h, t: (0, 0)),
                  pl.BlockSpec((4, HEAD_QK), lambda b, h, t: (0, 0)),
                  pl.BlockSpec((1, HEAD_V), lambda b, h, t: (0, 0))],
        out_specs=pl.BlockSpec((tq, LANES), lambda b, h, t: (b * nq + t, h)),
        out_shape=jax.ShapeDtypeStruct((n * seq, D_A), BF16),
        scratch_shapes=[pltpu.VMEM((past + seq, LANES), BF16), pltpu.VMEM((past + seq, LANES), BF16)],
        compiler_params=_cparams("parallel", "parallel", "arbitrary"),
        name="attn_sample",
    )(proj, proj, proj, cache_k, cache_v, cos_t, sin_t, cos_t, sin_t, diff_lambda,
      subln_g.reshape(1, HEAD_V))


def _rope_tables(seq):
    t = jnp.arange(seq)
    lane = jnp.arange(LANES)
    n_freq = HEAD_QK // 4
    inv = ROPE_BASE ** (-(lane % n_freq).astype(F32) / n_freq)
    use_col = (lane % HEAD_QK) >= HEAD_QK // 2
    pos = jnp.where(use_col[None, :], (t % GRID_W)[:, None], (t // GRID_W)[:, None]).astype(F32)
    ang = pos * inv[None, :]
    sin_signed = jnp.where((lane % (2 * n_freq) < n_freq)[None, :], -jnp.sin(ang), jnp.sin(ang))
    return jnp.cos(ang), sin_signed


CONV_ROWS = 32
CONV_LANES = 512


CONV_SPAN = BLK + 2 * CONV_HALO - SUBLANES


def _conv_ln_kernel(u_ref, up_ref, un_ref, w_ref, b_ref, g_ref, beta_ref, o_ref, pad, tmp, shifted, *, lay):
    i = pl.program_id(0)
    first, last = lay.seq_pos(i)
    d = u_ref.shape[1]
    n_ct = d // CONV_LANES
    for ct in range(n_ct):
        sl = slice(ct * CONV_LANES, (ct + 1) * CONV_LANES)
        pad[ct, 0:CONV_HALO, :] = jnp.where(first, 0.0, up_ref[:, sl])
        pad[ct, CONV_HALO:CONV_HALO + BLK, :] = u_ref[:, sl]
        pad[ct, CONV_HALO + BLK:, :] = jnp.where(last, 0.0, un_ref[:, sl])
    off = CONV_HALO - CONV_PAD

    def conv_tile(ct, carry):
        for s in range(SUBLANES):
            shifted[s] = pad[ct, s:s + CONV_SPAN, :]
        for c in range(BLK // CONV_ROWS):
            r0 = c * CONV_ROWS
            acc = jnp.zeros((CONV_ROWS, CONV_LANES), F32) + b_ref[ct]
            for j in range(CONV_W):
                q, s = divmod(off + j, SUBLANES)
                a0 = r0 + q * SUBLANES
                acc = acc + shifted[s, a0:a0 + CONV_ROWS, :] * w_ref[ct, j:j + 1, :]
            tmp[ct, r0:r0 + CONV_ROWS, :] = acc
        return carry

    lax.fori_loop(0, n_ct, conv_tile, 0)

    def norm_rows(c, carry):
        r0 = pl.multiple_of(c * CONV_ROWS, CONV_ROWS)
        xs = [tmp[ct, pl.ds(r0, CONV_ROWS), :] for ct in range(n_ct)]
        mu = sum(jnp.sum(x, axis=-1, keepdims=True) for x in xs) * (1.0 / d)
        xs = [x - mu for x in xs]
        var = sum(jnp.sum(x * x, axis=-1, keepdims=True) for x in xs) * (1.0 / d)
        inv = lax.rsqrt(var + LN_EPS)
        for ct, x in enumerate(xs):
            sl = slice(ct * CONV_LANES, (ct + 1) * CONV_LANES)
            y = x * inv * g_ref[:, sl] + beta_ref[:, sl]
            o_ref[pl.ds(r0, CONV_ROWS), sl] = (y * jax.nn.sigmoid(y)).astype(o_ref.dtype)
        return carry

    lax.fori_loop(0, BLK // CONV_ROWS, norm_rows, 0)


def _conv_ln(lay, u, w_dw, b_dw, cln_g, cln_b):
    d = u.shape[1]
    hb = BLK // CONV_HALO
    n_halo = lay.rows // CONV_HALO
    n_ct = d // CONV_LANES
    vec = pl.BlockSpec((1, d), lambda i: (0, 0))
    w_t = w_dw.reshape(CONV_W, n_ct, CONV_LANES).transpose(1, 0, 2)
    return pl.pallas_call(
        functools.partial(_conv_ln_kernel, lay=lay),
        grid=(lay.nb,),
        in_specs=[pl.BlockSpec((BLK, d), lambda i: (i, 0)),
                  pl.BlockSpec((CONV_HALO, d), lambda i: (jnp.maximum(i * hb - 1, 0), 0)),
                  pl.BlockSpec((CONV_HALO, d), lambda i: (jnp.minimum((i + 1) * hb, n_halo - 1), 0)),
                  pl.BlockSpec((n_ct, CONV_W, CONV_LANES), lambda i: (0, 0, 0)),
                  pl.BlockSpec((n_ct, 1, CONV_LANES), lambda i: (0, 0, 0)), vec, vec],
        out_specs=pl.BlockSpec((BLK, d), lambda i: (i, 0)),
        out_shape=jax.ShapeDtypeStruct(u.shape, BF16),
        scratch_shapes=[pltpu.VMEM((n_ct, BLK + 2 * CONV_HALO, CONV_LANES), F32),
                        pltpu.VMEM((n_ct, BLK, CONV_LANES), F32),
                        pltpu.VMEM((SUBLANES, CONV_SPAN, CONV_LANES), F32)],
        compiler_params=_cparams("parallel"),
        name="conv_ln",
    )(u, u, u, w_t, b_dw.reshape(n_ct, 1, CONV_LANES), cln_g.reshape(1, d), cln_b.reshape(1, d))


def _pad_cols(x, n):
    return jnp.pad(x, ((0, 0), (0, n - x.shape[1])))


def _rest_layout(x):
    c = 3 * D_B
    return jnp.concatenate([x[:, :c], _pad_cols(x[:, c:c + LORA_W], LANES),
                            _pad_cols(x[:, c + LORA_W:c + LORA_W + LORA_A], LANES),
                            x[:, c + LORA_W + LORA_A:]], axis=1)


def _conv_ffn(lay, x, g, mod, w_up, w_dw, w_down, tm, tm_small):
    h = _norm_mod(lay, x, g, mod, 3)
    u = _ffn_up(lay, h, w_up[:, :D_FF].astype(BF16), w_up[:, D_FF:].astype(BF16),
                w_dw[:, :D_FF], w_dw[:, D_FF:], tm, 512)
    return _mm_res(lay, [u], w_down.astype(BF16), x, mod, 5, tm_small, 512)


def _pair_state(s):
    b = s.shape[0]
    return s.reshape(b, N_PAIR, 2, HEAD_B, HEAD_B).transpose(0, 1, 3, 2, 4).reshape(b, N_PAIR, HEAD_B, LANES)


def _unpair_state(s):
    b = s.shape[0]
    return s.reshape(b, N_PAIR, HEAD_B, 2, HEAD_B).transpose(0, 1, 3, 2, 4).reshape(b, H_B, HEAD_B, HEAD_B)


def kernel(x_prompt, x_sample, cache_k, cache_v, state_wkv_fwd, state_wkv_bwd, c, c_ctx, w_ada, b_ada, norm_g, final_norm_g, w_in, w_out, diff_lambda, subln_g, shift_mu, w0, w2, a0, a2, g2, k_k, k_a, r_k, lnx_g, lnx_b, w_pw1, w_dw, b_dw, cln_g, cln_b, w_pw2, w_up, w_ffn_dw, w_down):
    n_prompt, seq_p, d = x_prompt.shape
    n_sample, seq_s, _ = x_sample.shape
    assert seq_p == BLK and seq_s % BLK == 0 and d == D_MODEL
    lay = _Layout(n_prompt, n_sample, seq_s)
    tm = lay.tile_rows()
    tm_small = min(tm, 512)

    x = jnp.concatenate([x_prompt.reshape(-1, d), x_sample.reshape(-1, d)], axis=0)
    c16 = jnp.concatenate([c_ctx[None, :], c, jnp.zeros((N_COND - 1 - n_sample, d), F32)], axis=0)
    mod = _adaln(c16, w_ada, b_ada)
    mod = mod.reshape(mod.shape[0], N_COND, 1, 6 * d)
    e3 = _seg_ones()

    e = 0
    w_rest = w_in[e][:, 2 * D_A + D_A:]
    w_in_p = jnp.concatenate([_rest_layout(w_rest), w_in[e][:, :3 * D_A]], axis=1).astype(BF16)
    mu_p = _rest_layout(shift_mu[e])
    pad_rows = lambda w: jnp.pad(w, ((0, 0), (0, LANES - w.shape[1]), (0, 0))).astype(BF16)

    h = _norm_mod(lay, x, norm_g[0, 0], mod[0], 0)
    proj = _mm(h, w_in_p, tm, 512)
    oa_p, new_k, new_v = _attn_prompt(lay, proj, diff_lambda[e], subln_g[e])
    cos_t, sin_t = _rope_tables(seq_s)
    oa_s = _attn_sample(lay, proj, cache_k[:, e:e + 1], cache_v[:, e:e + 1], cos_t, sin_t,
                        diff_lambda[e], subln_g[e])
    o_a = jnp.concatenate([oa_p, oa_s], axis=0)

    r_, v_, a_, wf, kf, bf, wb, kb, bb, g_, bonus = _rwkv_prep(
        lay, proj, mu_p, w0[e], pad_rows(w2[e]), a0[e], pad_rows(a2[e]), g2[e].astype(BF16),
        k_k[e].reshape(1, D_B), k_a[e].reshape(1, D_B), r_k[e].reshape(1, D_B), e3)
    init = jnp.stack([_pair_state(state_wkv_fwd[:, e]), _pair_state(state_wkv_bwd[:, e])], axis=1)
    yf, yb, fin = _rwkv_scan(lay, r_, v_, a_, wf, kf, bf, wb, kb, bb, init, _seg_ones_pair())
    o_b = _rwkv_post(lay, yf, yb, bonus, g_, lnx_g[e].reshape(1, D_B), lnx_b[e].reshape(1, D_B), e3)
    new_sf = _unpair_state(fin[:n_prompt, 0])[:, None]
    new_sb = _unpair_state(fin[:n_prompt, 1])[:, None]

    x = _mm_res(lay, [o_a, o_b], w_out[e].astype(BF16), x, mod[0], 2, tm_small, 512)
    x = _conv_ffn(lay, x, norm_g[0, 1], mod[0], w_up[0], w_ffn_dw[0], w_down[0], tm, tm_small)

    o = 0
    h = _norm_mod(lay, x, norm_g[1, 0], mod[1], 0)
    u = _mm_glu(h, w_pw1[o].astype(BF16), tm, 512)
    u = _conv_ln(lay, u, w_dw[o], b_dw[o], cln_g[o], cln_b[o])
    x = _mm_res(lay, [u], w_pw2[o].astype(BF16), x, mod[1], 2, tm_small, 512)
    x = _conv_ffn(lay, x, norm_g[1, 1], mod[1], w_up[1], w_ffn_dw[1], w_down[1], tm, tm_small)

    y_p = _final_norm(x, final_norm_g, 0, lay.pb).reshape(n_prompt, seq_p, d)
    y_s = _final_norm(x, final_norm_g, lay.pb, lay.nb - lay.pb).reshape(n_sample, seq_s, d)
    return (y_p, y_s, new_k, new_v, new_sf, new_sb)
```

```python
import functools
import math

import jax
import jax.numpy as jnp
from jax import lax
from jax.experimental import pallas as pl
from jax.experimental.pallas import tpu as pltpu

F32 = jnp.float32
BF16 = jnp.bfloat16

LANES = 128
SUBLANES = 8
VMEM_LIMIT_BYTES = 56 * 1024 * 1024

D_MODEL = 2048
BLK = 256
GRID_W = 64
HEAD_QK = 64
HEAD_V = 128
H_A = 8
D_A = 1024
HEAD_B = 64
H_B = 16
D_B = 1024
N_PAIR = D_B // LANES
LORA_W = 96
LORA_A = 96
LORA_G = 256
D_REST = 3 * D_B + 2 * LANES + LORA_G
D_PROJ = D_REST + 3 * D_A
D_FF = 5504
CONV_W = 31
CONV_PAD = (CONV_W - 1) // 2
CONV_HALO = 16
ATTN_Q_ROWS = 512
RMS_EPS = 1e-6
LN_EPS = 1e-5
GN_EPS = 64e-5
ROPE_BASE = 10000.0
LAM_INIT0 = 0.8 - 0.6 * math.exp(-0.3 * 0)
N_SPLIT = 2
N_COND = 16


def _cparams(*sem):
    return pltpu.CompilerParams(dimension_semantics=sem, vmem_limit_bytes=VMEM_LIMIT_BYTES)


class _Layout:
    def __init__(self, n_prompt, n_sample, sample_len):
        self.n_prompt = n_prompt
        self.n_sample = n_sample
        self.bps = sample_len // BLK
        self.pb = n_prompt
        self.nb = n_prompt + n_sample * self.bps
        self.rows = self.nb * BLK
        assert 1 + n_sample <= N_COND

    def cond_of_block(self, i):
        return jnp.where(i < self.pb, 0, 1 + (i - self.pb) // self.bps)

    def seq_pos(self, i):
        c = (i - self.pb) % self.bps
        first = jnp.logical_or(i < self.pb, c == 0)
        last = jnp.logical_or(i < self.pb, c == self.bps - 1)
        return first, last

    def tile_rows(self):
        t = self.bps * BLK
        while (self.pb * BLK) % t:
            t //= 2
        return t


def _adaln_kernel(c_ref, w_ref, b_ref, o_ref):
    x = c_ref[...]
    s = (x * jax.nn.sigmoid(x)).astype(BF16)
    o_ref[0] = jnp.dot(s, w_ref[0].astype(BF16), preferred_element_type=F32) + b_ref[0]


def _adaln(c16, w_ada, b_ada):
    depth, d, n = w_ada.shape
    tn = 1024
    return pl.pallas_call(
        _adaln_kernel,
        grid=(depth, n // tn),
        in_specs=[pl.BlockSpec((N_COND, d), lambda l, j: (0, 0)),
                  pl.BlockSpec((1, d, tn), lambda l, j: (l, 0, j)),
                  pl.BlockSpec((1, 1, tn), lambda l, j: (l, 0, j))],
        out_specs=pl.BlockSpec((1, N_COND, tn), lambda l, j: (l, 0, j)),
        out_shape=jax.ShapeDtypeStruct((depth, N_COND, n), F32),
        compiler_params=_cparams("parallel", "parallel"),
        name="adaln",
    )(c16, w_ada, b_ada.reshape(depth, 1, n))


def _norm_mod_kernel(x_ref, g_ref, sh_ref, sc_ref, o_ref):
    x = x_ref[...]
    y = x * lax.rsqrt(jnp.mean(x * x, axis=-1, keepdims=True) + RMS_EPS) * g_ref[...]
    o_ref[...] = (y * (1.0 + sc_ref[0]) + sh_ref[0]).astype(o_ref.dtype)


def _norm_mod(lay, x, g, mod, shift_idx):
    d = x.shape[1]
    cond = lay.cond_of_block
    return pl.pallas_call(
        _norm_mod_kernel,
        grid=(lay.nb,),
        in_specs=[pl.BlockSpec((BLK, d), lambda i: (i, 0)),
                  pl.BlockSpec((1, d), lambda i: (0, 0)),
                  pl.BlockSpec((1, 1, d), lambda i: (cond(i), 0, shift_idx)),
                  pl.BlockSpec((1, 1, d), lambda i: (cond(i), 0, shift_idx + 1))],
        out_specs=pl.BlockSpec((BLK, d), lambda i: (i, 0)),
        out_shape=jax.ShapeDtypeStruct(x.shape, BF16),
        compiler_params=_cparams("parallel"),
        name="norm_mod",
    )(x, g.reshape(1, d), mod, mod)


def _final_norm_kernel(x_ref, g_ref, o_ref):
    x = x_ref[...]
    o_ref[...] = x * lax.rsqrt(jnp.mean(x * x, axis=-1, keepdims=True) + RMS_EPS) * g_ref[...]


def _final_norm(x, g, blk0, nblk):
    d = x.shape[1]
    return pl.pallas_call(
        _final_norm_kernel,
        grid=(nblk,),
        in_specs=[pl.BlockSpec((BLK, d), lambda i: (i + blk0, 0)),
                  pl.BlockSpec((1, d), lambda i: (0, 0))],
        out_specs=pl.BlockSpec((BLK, d), lambda i: (i, 0)),
        out_shape=jax.ShapeDtypeStruct((nblk * BLK, d), F32),
        compiler_params=_cparams("parallel"),
        name="final_norm",
    )(x, g.reshape(1, d))


def _mm_kernel(a_ref, w_ref, o_ref):
    o_ref[...] = jnp.dot(a_ref[...], w_ref[...], preferred_element_type=F32)


def _mm(a, w, tm, tn):
    m, k = a.shape
    n = w.shape[1]
    return pl.pallas_call(
        _mm_kernel,
        grid=(m // tm, n // tn),
        in_specs=[pl.BlockSpec((tm, k), lambda i, j: (i, 0)),
                  pl.BlockSpec((k, tn), lambda i, j: (0, j))],
        out_specs=pl.BlockSpec((tm, tn), lambda i, j: (i, j)),
        out_shape=jax.ShapeDtypeStruct((m, n), F32),
        compiler_params=_cparams("parallel", "parallel"),
        name="mm",
    )(a, w)


def _mm_res_kernel(*refs, n_in):
    a_refs, w_refs = refs[:n_in], refs[n_in:2 * n_in]
    x_ref, g_ref, o_ref = refs[2 * n_in:]
    acc = jnp.dot(a_refs[0][...], w_refs[0][...], preferred_element_type=F32)
    for a_ref, w_ref in zip(a_refs[1:], w_refs[1:]):
        acc += jnp.dot(a_ref[...], w_ref[...], preferred_element_type=F32)
    o_ref[...] = x_ref[...] + g_ref[0] * acc


def _mm_res(lay, a_list, w, x, mod, gate_idx, tm, tn):
    m, n = x.shape
    bpt = tm // BLK
    cond = lambda i: lay.cond_of_block(i * bpt)
    ka = a_list[0].shape[1]
    assert all(a.shape[1] == ka for a in a_list) and w.shape[0] == ka * len(a_list)
    in_specs = ([pl.BlockSpec((tm, ka), lambda i, j: (i, 0)) for _ in a_list]
                + [pl.BlockSpec((ka, tn), functools.partial(lambda i, j, r: (r, j), r=r))
                   for r in range(len(a_list))]
                + [pl.BlockSpec((tm, tn), lambda i, j: (i, j)),
                   pl.BlockSpec((1, 1, tn), lambda i, j: (cond(i), 0, gate_idx * (n // tn) + j))])
    return pl.pallas_call(
        functools.partial(_mm_res_kernel, n_in=len(a_list)),
        grid=(m // tm, n // tn),
        in_specs=in_specs,
        out_specs=pl.BlockSpec((tm, tn), lambda i, j: (i, j)),
        out_shape=jax.ShapeDtypeStruct((m, n), F32),
        compiler_params=_cparams("parallel", "parallel"),
        name="mm_res",
    )(*a_list, *([w] * len(a_list)), x, mod)


def _mm_glu_kernel(a_ref, wa_ref, wb_ref, o_ref):
    a = a_ref[...]
    u = jnp.dot(a, wa_ref[...], preferred_element_type=F32)
    v = jnp.dot(a, wb_ref[...], preferred_element_type=F32)
    o_ref[...] = u * jax.nn.sigmoid(v)


def _mm_glu(a, w, tm, tn):
    m, k = a.shape
    n = w.shape[1] // 2
    nj = n // tn
    return pl.pallas_call(
        _mm_glu_kernel,
        grid=(m // tm, nj),
        in_specs=[pl.BlockSpec((tm, k), lambda i, j: (i, 0)),
                  pl.BlockSpec((k, tn), lambda i, j: (0, j)),
                  pl.BlockSpec((k, tn), lambda i, j: (0, j + nj))],
        out_specs=pl.BlockSpec((tm, tn), lambda i, j: (i, j)),
        out_shape=jax.ShapeDtypeStruct((m, n), F32),
        compiler_params=_cparams("parallel", "parallel"),
        name="mm_glu",
    )(a, w, w)


def _row_index(shape):
    return lax.broadcasted_iota(jnp.int32, shape, 0)


def _ffn_up_kernel(a_ref, wg_ref, wv_ref, dg_ref, dv_ref, o_ref, *, lay, tm):
    i = pl.program_id(0)
    seq_len = jnp.where(i * (tm // BLK) < lay.pb, BLK, lay.bps * BLK)
    a = a_ref[...]

    def conv(w_ref, d_ref):
        u = jnp.dot(a, w_ref[...], preferred_element_type=F32)
        pos = _row_index(u.shape) & (seq_len - 1)
        prev = jnp.where(pos == 0, 0.0, pltpu.roll(u, 1, axis=0))
        nxt = jnp.where(pos == seq_len - 1, 0.0, pltpu.roll(u, tm - 1, axis=0))
        return prev * d_ref[0:1, :] + u * d_ref[1:2, :] + nxt * d_ref[2:3, :]

    gate = conv(wg_ref, dg_ref)
    val = conv(wv_ref, dv_ref)
    o_ref[...] = (gate * jax.nn.sigmoid(gate) * val).astype(o_ref.dtype)


def _ffn_up(lay, a, w_gate, w_val, dw_gate, dw_val, tm, tn):
    m, k = a.shape
    f = w_gate.shape[1]
    w_spec = pl.BlockSpec((k, tn), lambda i, j: (0, j))
    dw_spec = pl.BlockSpec((3, tn), lambda i, j: (0, j))
    return pl.pallas_call(
        functools.partial(_ffn_up_kernel, lay=lay, tm=tm),
        grid=(m // tm, pl.cdiv(f, tn)),
        in_specs=[pl.BlockSpec((tm, k), lambda i, j: (i, 0)), w_spec, w_spec, dw_spec, dw_spec],
        out_specs=pl.BlockSpec((tm, tn), lambda i, j: (i, j)),
        out_shape=jax.ShapeDtypeStruct((m, f), BF16),
        compiler_params=_cparams("parallel", "parallel"),
        name="ffn_up",
    )(a, w_gate, w_val, dw_gate, dw_val)


def _split_bf16(x):
    parts = []
    r = x
    for s in range(N_SPLIT):
        p = r.astype(BF16)
        parts.append(p)
        if s + 1 < N_SPLIT:
            r = r - p.astype(F32)
    return parts


def _seg_sum(x, e_ref):
    lhs = jnp.concatenate(_split_bf16(x), axis=1)
    return jnp.dot(lhs, e_ref[...], preferred_element_type=F32)


def _seg_ones():
    r = jnp.arange(N_SPLIT * LANES)[:, None] % LANES
    c = jnp.arange(LANES)[None, :]
    return (r // HEAD_B == c // HEAD_B).astype(BF16)


def _seg_ones_pair():
    r = jnp.arange(2 * LANES)[:, None]
    c = jnp.arange(2 * LANES)[None, :]
    return (r // HEAD_B == c // HEAD_B).astype(BF16)


def _softplus(z):
    return jnp.maximum(z, 0.0) + jnp.log(1.0 + jnp.exp(-jnp.abs(z)))


def _prep_kernel(p_ref, pp_ref, pn_ref, mu_ref, w0_ref, w2_ref, a0_ref, a2_ref, g2_ref,
                 kk_ref, ka_ref, rk_ref, e_ref,
                 r_o, v_o, a_o, wf_o, kf_o, bf_o, wb_o, kb_o, bb_o, g_o, bonus_o, *, lay):
    i = pl.program_id(0)
    first, last = lay.seq_pos(i)
    rows = _row_index((BLK, 1))

    def shifted(c0, c1):
        p = p_ref[:, c0:c1]
        prev_row = jnp.where(first, 0.0, pp_ref[SUBLANES - 1:SUBLANES, c0:c1])
        next_row = jnp.where(last, 0.0, pn_ref[0:1, c0:c1])
        p_prev = jnp.where(rows == 0, prev_row, pltpu.roll(p, 1, axis=0))
        p_next = jnp.where(rows == BLK - 1, next_row, pltpu.roll(p, BLK - 1, axis=0))
        return p + mu_ref[0:1, c0:c1] * (p_prev - p) + mu_ref[1:2, c0:c1] * (p_next - p)

    c_w = 3 * D_B
    c_a = c_w + LANES
    c_g = c_a + LANES
    tw = jnp.tanh(shifted(c_w, c_a)).astype(BF16)
    xa = shifted(c_a, c_g).astype(BF16)
    sg = jax.nn.sigmoid(shifted(c_g, D_REST)).astype(BF16)

    for j in range(N_PAIR):
        sl = slice(j * LANES, (j + 1) * LANES)
        r = shifted(j * LANES, (j + 1) * LANES)
        kb = shifted(D_B + j * LANES, D_B + (j + 1) * LANES)
        vb = shifted(2 * D_B + j * LANES, 2 * D_B + (j + 1) * LANES)
        r_o[:, sl] = r
        v_o[:, sl] = vb
        g_o[:, sl] = jnp.dot(sg, g2_ref[:, sl], preferred_element_type=F32)
        kk = kb * kk_ref[:, sl]
        nrm = jnp.sqrt(_seg_sum(kk * kk, e_ref))
        kk = kk / jnp.maximum(nrm, 1e-12)
        a_o[:, sl] = -kk
        bonus_o[:, sl] = _seg_sum(r * kb * rk_ref[:, sl], e_ref) * vb
        for d, (w_o, k_o, b_o) in enumerate(((wf_o, kf_o, bf_o), (wb_o, kb_o, bb_o))):
            lw = w0_ref[d:d + 1, sl] + jnp.dot(tw, w2_ref[d, :, sl], preferred_element_type=F32)
            w_log = -_softplus(-lw) - 0.5
            w_o[:, sl] = jnp.exp(-jnp.exp(w_log))
            a_lr = jax.nn.sigmoid(a0_ref[d:d + 1, sl]
                                  + jnp.dot(xa, a2_ref[d, :, sl], preferred_element_type=F32))
            k_o[:, sl] = kb * (1.0 + (a_lr - 1.0) * ka_ref[:, sl])
            b_o[:, sl] = kk * a_lr


def _rwkv_prep(lay, proj, mu, w0, w2p, a0, a2p, g2, k_k, k_a, r_k, e3):
    nb = lay.nb
    hb = BLK // SUBLANES
    n_halo = lay.rows // SUBLANES
    full = lambda shape: pl.BlockSpec(shape, lambda i: (0,) * len(shape))
    out = jax.ShapeDtypeStruct((lay.rows, D_B), F32)
    return pl.pallas_call(
        functools.partial(_prep_kernel, lay=lay),
        grid=(nb,),
        in_specs=[pl.BlockSpec((BLK, D_REST), lambda i: (i, 0)),
                  pl.BlockSpec((SUBLANES, D_REST), lambda i: (jnp.maximum(i * hb - 1, 0), 0)),
                  pl.BlockSpec((SUBLANES, D_REST), lambda i: (jnp.minimum((i + 1) * hb, n_halo - 1), 0)),
                  full((2, D_REST)), full((2, D_B)), full((2, LANES, D_B)), full((2, D_B)),
                  full((2, LANES, D_B)), full((LORA_G, D_B)), full((1, D_B)), full((1, D_B)),
                  full((1, D_B)), full((N_SPLIT * LANES, LANES))],
        out_specs=[pl.BlockSpec((BLK, D_B), lambda i: (i, 0))] * 11,
        out_shape=[out] * 11,
        compiler_params=_cparams("parallel"),
        name="rwkv_prep",
    )(proj, proj, proj, mu, w0, w2p, a0, a2p, g2, k_k, k_a, r_k, e3)


def _scan_kernel(rf, vf, af, wf, kf, bf, rb, vb, ab, wb, kb, bb, init_ref, e_ref,
                 yf_ref, yb_ref, fin_ref, st, ybuf, *, lay):
    i = pl.program_id(0)
    first, _ = lay.seq_pos(i)

    @pl.when(first)
    def _():
        st[...] = jnp.where(i >= lay.pb, init_ref[0], 0.0)

    diag = (lax.broadcasted_iota(jnp.int32, (HEAD_B, LANES), 0)
            == lax.broadcasted_iota(jnp.int32, (HEAD_B, LANES), 1) % HEAD_B)
    dirs = ((rf, vf, af, wf, kf, bf, yf_ref), (rb, vb, ab, wb, kb, bb, yb_ref))

    def seg_sums(tiles):
        bt = [t.astype(BF16) for t in tiles]
        lhs = jnp.concatenate([jnp.concatenate(bt[k:k + 2], axis=1) for k in range(0, len(bt), 2)], axis=0)
        res = jnp.dot(lhs, e_ref[...], preferred_element_type=F32)
        return [res[(k // 2) * HEAD_B:(k // 2 + 1) * HEAD_B, (k % 2) * LANES:(k % 2 + 1) * LANES]
                for k in range(len(bt))]

    def group(gi, carry):
        bases = (pl.multiple_of(gi * SUBLANES, SUBLANES),
                 pl.multiple_of(BLK - SUBLANES - gi * SUBLANES, SUBLANES))

        def row(ref, d, j, sl):
            return ref[pl.ds(bases[d], SUBLANES), sl][j:j + 1, :]

        def emit_y(js, tiles):
            for d in range(2):
                for p in range(N_PAIR):
                    ybuf[d, js[d]:js[d] + 1, p * LANES:(p + 1) * LANES] = jnp.sum(
                        jnp.where(diag, tiles[d * N_PAIR + p], 0.0), axis=0, keepdims=True)

        js_prev = None
        for t in range(SUBLANES):
            js = (t, SUBLANES - 1 - t)
            xs, ys = [], []
            for d in range(2):
                r_r, v_r, a_r = dirs[d][0], dirs[d][1], dirs[d][2]
                for p in range(N_PAIR):
                    sl = slice(p * LANES, (p + 1) * LANES)
                    s_cur = st[d, p]
                    xs.append(s_cur * row(a_r, d, js[d], sl))
                    xs.append(jnp.where(diag, row(v_r, d, js[d], sl), 0.0))
                    if js_prev is not None:
                        ys.append(s_cur * row(r_r, d, js_prev[d], sl))
            res = seg_sums(xs + ys)
            if js_prev is not None:
                emit_y(js_prev, res[len(xs):])
            for d in range(2):
                w_r, k_r, b_r = dirs[d][3], dirs[d][4], dirs[d][5]
                for p in range(N_PAIR):
                    sl = slice(p * LANES, (p + 1) * LANES)
                    idx = d * N_PAIR + p
                    sa, vcol = res[2 * idx], res[2 * idx + 1]
                    st[d, p] = (st[d, p] * row(w_r, d, js[d], sl) + sa * row(b_r, d, js[d], sl)
                                + vcol * row(k_r, d, js[d], sl))
            js_prev = js
        emit_y(js_prev, seg_sums([st[d, p] * row(dirs[d][0], d, js_prev[d], slice(p * LANES, (p + 1) * LANES))
                                  for d in range(2) for p in range(N_PAIR)]))
        for d in range(2):
            dirs[d][6][pl.ds(bases[d], SUBLANES), :] = ybuf[d]
        return carry

    lax.fori_loop(0, BLK // SUBLANES, group, 0)
    fin_ref[0] = st[...]


def _rwkv_scan(lay, r, v, a, wf, kf, bf, wb, kb, bb, init, e3):
    pb, bps = lay.pb, lay.bps

    def bwd_block(i):
        c = (i - pb) % bps
        return jnp.where(i < pb, i, i - c + (bps - 1 - c))

    fwd = pl.BlockSpec((BLK, D_B), lambda i: (i, 0))
    bwd = pl.BlockSpec((BLK, D_B), lambda i: (bwd_block(i), 0))
    st_shape = (2, N_PAIR, HEAD_B, LANES)
    y = jax.ShapeDtypeStruct((lay.rows, D_B), F32)
    return pl.pallas_call(
        functools.partial(_scan_kernel, lay=lay),
        grid=(lay.nb,),
        in_specs=[fwd] * 6 + [bwd] * 6
        + [pl.BlockSpec((1,) + st_shape, lambda i: (jnp.maximum(i - pb, 0) // bps, 0, 0, 0, 0)),
           pl.BlockSpec((2 * LANES, 2 * LANES), lambda i: (0, 0))],
        out_specs=[fwd, bwd, pl.BlockSpec((1,) + st_shape, lambda i: (i, 0, 0, 0, 0))],
        out_shape=[y, y, jax.ShapeDtypeStruct((lay.nb,) + st_shape, F32)],
        scratch_shapes=[pltpu.VMEM(st_shape, F32), pltpu.VMEM((2, SUBLANES, D_B), F32)],
        compiler_params=_cparams("arbitrary"),
        name="rwkv_scan",
    )(r, v, a, wf, kf, bf, r, v, a, wb, kb, bb, init, e3)


def _post_kernel(yf_ref, yb_ref, bonus_ref, g_ref, lg_ref, lb_ref, e_ref, o_ref):
    for j in range(N_PAIR):
        sl = slice(j * LANES, (j + 1) * LANES)
        y = yf_ref[:, sl] + yb_ref[:, sl]
        mu = _seg_sum(y, e_ref) * (1.0 / HEAD_B)
        dlt = y - mu
        var = _seg_sum(dlt * dlt, e_ref) * (1.0 / HEAD_B)
        yn = dlt * lax.rsqrt(var + GN_EPS) * lg_ref[:, sl] + lb_ref[:, sl]
        o_ref[:, sl] = ((yn + bonus_ref[:, sl]) * g_ref[:, sl]).astype(o_ref.dtype)


def _rwkv_post(lay, yf, yb, bonus, g, lnx_g, lnx_b, e3):
    blk = pl.BlockSpec((BLK, D_B), lambda i: (i, 0))
    vec = pl.BlockSpec((1, D_B), lambda i: (0, 0))
    return pl.pallas_call(
        _post_kernel,
        grid=(lay.nb,),
        in_specs=[blk, blk, blk, blk, vec, vec, pl.BlockSpec((N_SPLIT * LANES, LANES), lambda i: (0, 0))],
        out_specs=blk,
        out_shape=jax.ShapeDtypeStruct((lay.rows, D_B), BF16),
        compiler_params=_cparams("parallel"),
        name="rwkv_post",
    )(yf, yb, bonus, g, lnx_g, lnx_b, e3)


def _lambda_value(dl_ref):
    dl = dl_ref[...]
    s01 = jnp.sum(dl[0:1] * dl[1:2], axis=-1, keepdims=True)
    s23 = jnp.sum(dl[2:3] * dl[3:4], axis=-1, keepdims=True)
    return jnp.exp(s01) - jnp.exp(s23) + LAM_INIT0


def _softmax_rows(s):
    e = jnp.exp(s - jnp.max(s, axis=-1, keepdims=True))
    return e / jnp.sum(e, axis=-1, keepdims=True)


def _diff_attend(q, k_bf, v_bf, lam, g_row):
    lane = lax.broadcasted_iota(jnp.int32, q.shape, 1)
    nt = (((1,), (1,)), ((), ()))
    q1 = jnp.where(lane < HEAD_QK, q, 0.0).astype(BF16)
    q2 = jnp.where(lane >= HEAD_QK, q, 0.0).astype(BF16)
    p1 = _softmax_rows(lax.dot_general(q1, k_bf, nt, preferred_element_type=F32))
    p2 = _softmax_rows(lax.dot_general(q2, k_bf, nt, preferred_element_type=F32))
    a = (p1 - lam * p2).astype(BF16)
    o = jnp.dot(a, v_bf, preferred_element_type=F32)
    o = o * lax.rsqrt(jnp.mean(o * o, axis=-1, keepdims=True) + LN_EPS) * g_row
    return o * (1.0 - LAM_INIT0)


def _attn_prompt_kernel(q_ref, k_ref, v_ref, dl_ref, g_ref, o_ref, ck_ref, cv_ref):
    k = k_ref[...]
    v = v_ref[...]
    ck_ref[0, 0, 0, 0] = k[:, :HEAD_QK]
    ck_ref[0, 0, 0, 1] = k[:, HEAD_QK:]
    cv_ref[0, 0, 0] = v
    o = _diff_attend(q_ref[...], (k * HEAD_QK ** -0.5).astype(BF16), v.astype(BF16),
                     _lambda_value(dl_ref), g_ref[...])
    o_ref[...] = o.astype(o_ref.dtype)


def _attn_prompt(lay, proj, diff_lambda, subln_g):
    qb, kb, vb = D_REST // LANES, (D_REST + D_A) // LANES, (D_REST + 2 * D_A) // LANES
    n = lay.n_prompt
    return pl.pallas_call(
        _attn_prompt_kernel,
        grid=(n, H_A),
        in_specs=[pl.BlockSpec((BLK, LANES), lambda b, h: (b, qb + h)),
                  pl.BlockSpec((BLK, LANES), lambda b, h: (b, kb + h)),
                  pl.BlockSpec((BLK, LANES), lambda b, h: (b, vb + h)),
                  pl.BlockSpec((4, HEAD_QK), lambda b, h: (0, 0)),
                  pl.BlockSpec((1, HEAD_V), lambda b, h: (0, 0))],
        out_specs=[pl.BlockSpec((BLK, LANES), lambda b, h: (b, h)),
                   pl.BlockSpec((1, 1, 1, 2, BLK, HEAD_QK), lambda b, h: (b, 0, h, 0, 0, 0)),
                   pl.BlockSpec((1, 1, 1, BLK, HEAD_V), lambda b, h: (b, 0, h, 0, 0))],
        out_shape=[jax.ShapeDtypeStruct((n * BLK, D_A), BF16),
                   jax.ShapeDtypeStruct((n, 1, H_A, 2, BLK, HEAD_QK), F32),
                   jax.ShapeDtypeStruct((n, 1, H_A, BLK, HEAD_V), F32)],
        compiler_params=_cparams("parallel", "parallel"),
        name="attn_prompt",
    )(proj, proj, proj, diff_lambda, subln_g.reshape(1, HEAD_V))


def _rope(x, cos, sin_signed):
    lane = lax.broadcasted_iota(jnp.int32, x.shape, 1)
    partner = jnp.where(lane % 32 < 16, pltpu.roll(x, LANES - 16, axis=1), pltpu.roll(x, 16, axis=1))
    return x * cos + partner * sin_signed


def _attn_sample_kernel(q_ref, k_ref, v_ref, ck_ref, cv_ref, cq_ref, sq_ref, ck_t_ref, sk_t_ref,
                        dl_ref, g_ref, o_ref, k_all, v_all, *, past, seq):
    @pl.when(pl.program_id(2) == 0)
    def _():
        scale = HEAD_QK ** -0.5
        k_all[0:past, :] = (jnp.concatenate([ck_ref[0, 0, 0, 0], ck_ref[0, 0, 0, 1]], axis=1)
                            * scale).astype(BF16)
        k_all[past:past + seq, :] = (_rope(k_ref[...], ck_t_ref[...], sk_t_ref[...]) * scale).astype(BF16)
        v_all[0:past, :] = cv_ref[0, 0, 0].astype(BF16)
        v_all[past:past + seq, :] = v_ref[...].astype(BF16)

    q = _rope(q_ref[...], cq_ref[...], sq_ref[...])
    o = _diff_attend(q, k_all[...], v_all[...], _lambda_value(dl_ref), g_ref[...])
    o_ref[...] = o.astype(o_ref.dtype)


def _attn_sample(lay, proj, cache_k, cache_v, cos_t, sin_t, diff_lambda, subln_g):
    qb, kb, vb = D_REST // LANES, (D_REST + D_A) // LANES, (D_REST + 2 * D_A) // LANES
    n, bps, pb = lay.n_sample, lay.bps, lay.pb
    seq = bps * BLK
    past = cache_k.shape[4]
    s0 = pb // bps
    assert pb % bps == 0
    tq = ATTN_Q_ROWS if seq % ATTN_Q_ROWS == 0 else BLK
    nq = seq // tq
    q0 = pb * BLK // tq
    return pl.pallas_call(
        functools.partial(_attn_sample_kernel, past=past, seq=seq),
        grid=(n, H_A, nq),
        in_specs=[pl.BlockSpec((tq, LANES), lambda b, h, t: (q0 + b * nq + t, qb + h)),
                  pl.BlockSpec((seq, LANES), lambda b, h, t: (s0 + b, kb + h)),
                  pl.BlockSpec((seq, LANES), lambda b, h, t: (s0 + b, vb + h)),
                  pl.BlockSpec((1, 1, 1, 2, past, HEAD_QK), lambda b, h, t: (b, 0, h, 0, 0, 0)),
                  pl.BlockSpec((1, 1, 1, past, HEAD_V), lambda b, h, t: (b, 0, h, 0, 0)),
                  pl.BlockSpec((tq, LANES), lambda b, h, t: (t, 0)),
                  pl.BlockSpec((tq, LANES), lambda b, h, t: (t, 0)),
                  pl.BlockSpec((seq, LANES), lambda b, h, t: (0, 0)),
                  pl.BlockSpec((seq, LANES), lambda b, h, t: (0, 0)),
                  pl.BlockSpec((4, HEAD_QK), lambda b, h, t: (0, 0)),
                  pl.BlockSpec((1, HEAD_V), lambda b, h, t: (0, 0))],
        out_specs=pl.BlockSpec((tq, LANES), lambda b, h, t: (b * nq + t, h)),
        out_shape=jax.ShapeDtypeStruct((n * seq, D_A), BF16),
        scratch_shapes=[pltpu.VMEM((past + seq, LANES), BF16), pltpu.VMEM((past + seq, LANES), BF16)],
        compiler_params=_cparams("parallel", "parallel", "arbitrary"),
        name="attn_sample",
    )(proj, proj, proj, cache_k, cache_v, cos_t, sin_t, cos_t, sin_t, diff_lambda,
      subln_g.reshape(1, HEAD_V))


def _rope_tables(seq):
    t = jnp.arange(seq)
    lane = jnp.arange(LANES)
    n_freq = HEAD_QK // 4
    inv = ROPE_BASE ** (-(lane % n_freq).astype(F32) / n_freq)
    use_col = (lane % HEAD_QK) >= HEAD_QK // 2
    pos = jnp.where(use_col[None, :], (t % GRID_W)[:, None], (t // GRID_W)[:, None]).astype(F32)
    ang = pos * inv[None, :]
    sin_signed = jnp.where((lane % (2 * n_freq) < n_freq)[None, :], -jnp.sin(ang), jnp.sin(ang))
    return jnp.cos(ang), sin_signed


CONV_T = 8
CONV_UNROLL = 4
CONV_SUB = 16


def _conv_ln_kernel(u_ref, up_ref, un_ref, w_ref, b_ref, g_ref, beta_ref, o_ref, pad, *, lay):
    i = pl.program_id(0)
    first, last = lay.seq_pos(i)
    pad[0:CONV_HALO] = jnp.where(first, 0.0, up_ref[...])
    pad[CONV_HALO:CONV_HALO + BLK] = u_ref[...]
    pad[CONV_HALO + BLK:] = jnp.where(last, 0.0, un_ref[...])
    off = CONV_HALO - CONV_PAD
    n_ch = CONV_SUB * LANES

    def chunk(c, carry):
        t0 = c * CONV_T
        acc = jnp.zeros((CONV_T, CONV_SUB, LANES), F32) + b_ref[...]
        for j in range(CONV_W):
            acc = acc + pad[pl.ds(t0 + off + j, CONV_T)] * w_ref[j]
        tot = lambda x: jnp.sum(jnp.sum(x, axis=2, keepdims=True), axis=1, keepdims=True)
        xc = acc - tot(acc) * (1.0 / n_ch)
        inv = lax.rsqrt(tot(xc * xc) * (1.0 / n_ch) + LN_EPS)
        y = xc * inv * g_ref[...] + beta_ref[...]
        o_ref[pl.ds(t0, CONV_T)] = (y * jax.nn.sigmoid(y)).astype(o_ref.dtype)
        return carry

    lax.fori_loop(0, BLK // CONV_T, chunk, 0, unroll=CONV_UNROLL)


def _conv_ln(lay, u, w_dw, b_dw, cln_g, cln_b):
    rows, d = u.shape
    assert d == CONV_SUB * LANES
    hb = BLK // CONV_HALO
    n_halo = rows // CONV_HALO
    as3 = lambda x: x.reshape(x.shape[0], CONV_SUB, LANES)
    vec = pl.BlockSpec((1, CONV_SUB, LANES), lambda i: (0, 0, 0))
    out = pl.pallas_call(
        functools.partial(_conv_ln_kernel, lay=lay),
        grid=(lay.nb,),
        in_specs=[pl.BlockSpec((BLK, CONV_SUB, LANES), lambda i: (i, 0, 0)),
                  pl.BlockSpec((CONV_HALO, CONV_SUB, LANES), lambda i: (jnp.maximum(i * hb - 1, 0), 0, 0)),
                  pl.BlockSpec((CONV_HALO, CONV_SUB, LANES),
                               lambda i: (jnp.minimum((i + 1) * hb, n_halo - 1), 0, 0)),
                  pl.BlockSpec((CONV_W, CONV_SUB, LANES), lambda i: (0, 0, 0)), vec, vec, vec],
        out_specs=pl.BlockSpec((BLK, CONV_SUB, LANES), lambda i: (i, 0, 0)),
        out_shape=jax.ShapeDtypeStruct((rows, CONV_SUB, LANES), BF16),
        scratch_shapes=[pltpu.VMEM((BLK + 2 * CONV_HALO, CONV_SUB, LANES), F32)],
        compiler_params=_cparams("parallel"),
        name="conv_ln",
    )(as3(u), as3(u), as3(u), as3(w_dw), as3(b_dw[None]), as3(cln_g[None]), as3(cln_b[None]))
    return out.reshape(rows, d)


def _pad_cols(x, n):
    return jnp.pad(x, ((0, 0), (0, n - x.shape[1])))


def _rest_layout(x):
    c = 3 * D_B
    return jnp.concatenate([x[:, :c], _pad_cols(x[:, c:c + LORA_W], LANES),
                            _pad_cols(x[:, c + LORA_W:c + LORA_W + LORA_A], LANES),
                            x[:, c + LORA_W + LORA_A:]], axis=1)


def _conv_ffn(lay, x, g, mod, w_up, w_dw, w_down, tm, tm_small):
    h = _norm_mod(lay, x, g, mod, 3)
    u = _ffn_up(lay, h, w_up[:, :D_FF].astype(BF16), w_up[:, D_FF:].astype(BF16),
                w_dw[:, :D_FF], w_dw[:, D_FF:], tm, 512)
    return _mm_res(lay, [u], w_down.astype(BF16), x, mod, 5, tm, 512)


def _pair_state(s):
    b = s.shape[0]
    return s.reshape(b, N_PAIR, 2, HEAD_B, HEAD_B).transpose(0, 1, 3, 2, 4).reshape(b, N_PAIR, HEAD_B, LANES)


def _unpair_state(s):
    b = s.shape[0]
    return s.reshape(b, N_PAIR, HEAD_B, 2, HEAD_B).transpose(0, 1, 3, 2, 4).reshape(b, H_B, HEAD_B, HEAD_B)


def kernel(x_prompt, x_sample, cache_k, cache_v, state_wkv_fwd, state_wkv_bwd, c, c_ctx, w_ada, b_ada, norm_g, final_norm_g, w_in, w_out, diff_lambda, subln_g, shift_mu, w0, w2, a0, a2, g2, k_k, k_a, r_k, lnx_g, lnx_b, w_pw1, w_dw, b_dw, cln_g, cln_b, w_pw2, w_up, w_ffn_dw, w_down):
    n_prompt, seq_p, d = x_prompt.shape
    n_sample, seq_s, _ = x_sample.shape
    assert seq_p == BLK and seq_s % BLK == 0 and d == D_MODEL
    lay = _Layout(n_prompt, n_sample, seq_s)
    tm = lay.tile_rows()
    tm_small = min(tm, 512)

    x = jnp.concatenate([x_prompt.reshape(-1, d), x_sample.reshape(-1, d)], axis=0)
    c16 = jnp.concatenate([c_ctx[None, :], c, jnp.zeros((N_COND - 1 - n_sample, d), F32)], axis=0)
    mod = _adaln(c16, w_ada, b_ada)
    mod = mod.reshape(mod.shape[0], N_COND, 1, 6 * d)
    e3 = _seg_ones()

    e = 0
    w_in_b = w_in[e].astype(BF16)
    w_in_p = jnp.concatenate([_rest_layout(w_in_b[:, 3 * D_A:]), w_in_b[:, :3 * D_A]], axis=1)
    mu_p = _rest_layout(shift_mu[e])
    pad_rows = lambda w: jnp.pad(w, ((0, 0), (0, LANES - w.shape[1]), (0, 0))).astype(BF16)

    h = _norm_mod(lay, x, norm_g[0, 0], mod[0], 0)
    proj = _mm(h, w_in_p, tm, 512)
    oa_p, new_k, new_v = _attn_prompt(lay, proj, diff_lambda[e], subln_g[e])
    cos_t, sin_t = _rope_tables(seq_s)
    oa_s = _attn_sample(lay, proj, cache_k[:, e:e + 1], cache_v[:, e:e + 1], cos_t, sin_t,
                        diff_lambda[e], subln_g[e])
    o_a = jnp.concatenate([oa_p, oa_s], axis=0)

    r_, v_, a_, wf, kf, bf, wb, kb, bb, g_, bonus = _rwkv_prep(
        lay, proj, mu_p, w0[e], pad_rows(w2[e]), a0[e], pad_rows(a2[e]), g2[e].astype(BF16),
        k_k[e].reshape(1, D_B), k_a[e].reshape(1, D_B), r_k[e].reshape(1, D_B), e3)
    init = jnp.stack([_pair_state(state_wkv_fwd[:, e]), _pair_state(state_wkv_bwd[:, e])], axis=1)
    yf, yb, fin = _rwkv_scan(lay, r_, v_, a_, wf, kf, bf, wb, kb, bb, init, _seg_ones_pair())
    o_b = _rwkv_post(lay, yf, yb, bonus, g_, lnx_g[e].reshape(1, D_B), lnx_b[e].reshape(1, D_B), e3)
    new_sf = _unpair_state(fin[:n_prompt, 0])[:, None]
    new_sb = _unpair_state(fin[:n_prompt, 1])[:, None]

    x = _mm_res(lay, [o_a, o_b], w_out[e].astype(BF16), x, mod[0], 2, tm_small, 512)
    x = _conv_ffn(lay, x, norm_g[0, 1], mod[0], w_up[0], w_ffn_dw[0], w_down[0], tm, tm_small)

    o = 0
    h = _norm_mod(lay, x, norm_g[1, 0], mod[1], 0)
    u = _mm_glu(h, w_pw1[o].astype(BF16), tm, 512)
    u = _conv_ln(lay, u, w_dw[o], b_dw[o], cln_g[o], cln_b[o])
    x = _mm_res(lay, [u], w_pw2[o].astype(BF16), x, mod[1], 2, tm_small, 512)
    x = _conv_ffn(lay, x, norm_g[1, 1], mod[1], w_up[1], w_ffn_dw[1], w_down[1], tm, tm_small)

    y_p = _final_norm(x, final_norm_g, 0, lay.pb).reshape(n_prompt, seq_p, d)
    y_s = _final_norm(x, final_norm_g, lay.pb, lay.nb - lay.pb).reshape(n_sample, seq_s, d)
    return (y_p, y_s, new_k, new_v, new_sf, new_sb)
```

```python
import functools
import math

import jax
import jax.numpy as jnp
from jax import lax
from jax.experimental import pallas as pl
from jax.experimental.pallas import tpu as pltpu

F32 = jnp.float32
BF16 = jnp.bfloat16

LANES = 128
SUBLANES = 8
VMEM_LIMIT_BYTES = 56 * 1024 * 1024

D_MODEL = 2048
BLK = 256
GRID_W = 64
HEAD_QK = 64
HEAD_V = 128
H_A = 8
D_A = 1024
HEAD_B = 64
H_B = 16
D_B = 1024
N_PAIR = D_B // LANES
LORA_W = 96
LORA_A = 96
LORA_G = 256
D_REST = 3 * D_B + 2 * LANES + LORA_G
D_PROJ = D_REST + 3 * D_A
D_FF = 5504
CONV_W = 31
CONV_PAD = (CONV_W - 1) // 2
CONV_HALO = 16
ATTN_Q_ROWS = 512
NORM_ROWS = 512
RMS_EPS = 1e-6
LN_EPS = 1e-5
GN_EPS = 64e-5
ROPE_BASE = 10000.0
LAM_INIT0 = 0.8 - 0.6 * math.exp(-0.3 * 0)
N_SPLIT = 2
N_COND = 16


def _cparams(*sem):
    return pltpu.CompilerParams(dimension_semantics=sem, vmem_limit_bytes=VMEM_LIMIT_BYTES)


class _Layout:
    def __init__(self, n_prompt, n_sample, sample_len):
        self.n_prompt = n_prompt
        self.n_sample = n_sample
        self.bps = sample_len // BLK
        self.pb = n_prompt
        self.nb = n_prompt + n_sample * self.bps
        self.rows = self.nb * BLK
        assert 1 + n_sample <= N_COND

    def cond_of_block(self, i):
        return jnp.where(i < self.pb, 0, 1 + (i - self.pb) // self.bps)

    def seq_pos(self, i):
        c = (i - self.pb) % self.bps
        first = jnp.logical_or(i < self.pb, c == 0)
        last = jnp.logical_or(i < self.pb, c == self.bps - 1)
        return first, last

    def tile_rows(self):
        t = self.bps * BLK
        while (self.pb * BLK) % t:
            t //= 2
        return t


def _adaln_kernel(c_ref, w_ref, b_ref, o_ref):
    x = c_ref[...]
    s = (x * jax.nn.sigmoid(x)).astype(BF16)
    o_ref[0] = jnp.dot(s, w_ref[0].astype(BF16), preferred_element_type=F32) + b_ref[0]


def _adaln(c16, w_ada, b_ada):
    depth, d, n = w_ada.shape
    tn = 1024
    return pl.pallas_call(
        _adaln_kernel,
        grid=(depth, n // tn),
        in_specs=[pl.BlockSpec((N_COND, d), lambda l, j: (0, 0)),
                  pl.BlockSpec((1, d, tn), lambda l, j: (l, 0, j)),
                  pl.BlockSpec((1, 1, tn), lambda l, j: (l, 0, j))],
        out_specs=pl.BlockSpec((1, N_COND, tn), lambda l, j: (l, 0, j)),
        out_shape=jax.ShapeDtypeStruct((depth, N_COND, n), F32),
        compiler_params=_cparams("parallel", "parallel"),
        name="adaln",
    )(c16, w_ada, b_ada.reshape(depth, 1, n))


def _norm_mod_kernel(x_ref, g_ref, sh_ref, sc_ref, o_ref):
    x = x_ref[...]
    y = x * lax.rsqrt(jnp.mean(x * x, axis=-1, keepdims=True) + RMS_EPS) * g_ref[...]
    o_ref[...] = (y * (1.0 + sc_ref[0]) + sh_ref[0]).astype(o_ref.dtype)


def _norm_mod(lay, x, g, mod, shift_idx):
    d = x.shape[1]
    tr = min(lay.tile_rows(), NORM_ROWS)
    cond = lambda i: lay.cond_of_block(i * (tr // BLK))
    return pl.pallas_call(
        _norm_mod_kernel,
        grid=(lay.rows // tr,),
        in_specs=[pl.BlockSpec((tr, d), lambda i: (i, 0)),
                  pl.BlockSpec((1, d), lambda i: (0, 0)),
                  pl.BlockSpec((1, 1, d), lambda i: (cond(i), 0, shift_idx)),
                  pl.BlockSpec((1, 1, d), lambda i: (cond(i), 0, shift_idx + 1))],
        out_specs=pl.BlockSpec((tr, d), lambda i: (i, 0)),
        out_shape=jax.ShapeDtypeStruct(x.shape, BF16),
        compiler_params=_cparams("parallel"),
        name="norm_mod",
    )(x, g.reshape(1, d), mod, mod)


def _final_norm_kernel(x_ref, g_ref, o_ref):
    x = x_ref[...]
    o_ref[...] = x * lax.rsqrt(jnp.mean(x * x, axis=-1, keepdims=True) + RMS_EPS) * g_ref[...]


def _final_norm(x, g, blk0, nblk):
    d = x.shape[1]
    return pl.pallas_call(
        _final_norm_kernel,
        grid=(nblk,),
        in_specs=[pl.BlockSpec((BLK, d), lambda i: (i + blk0, 0)),
                  pl.BlockSpec((1, d), lambda i: (0, 0))],
        out_specs=pl.BlockSpec((BLK, d), lambda i: (i, 0)),
        out_shape=jax.ShapeDtypeStruct((nblk * BLK, d), F32),
        compiler_params=_cparams("parallel"),
        name="final_norm",
    )(x, g.reshape(1, d))


def _mm_kernel(a_ref, w_ref, o_ref):
    o_ref[...] = jnp.dot(a_ref[...], w_ref[...], preferred_element_type=F32)


def _mm(a, w, tm, tn):
    m, k = a.shape
    n = w.shape[1]
    return pl.pallas_call(
        _mm_kernel,
        grid=(m // tm, n // tn),
        in_specs=[pl.BlockSpec((tm, k), lambda i, j: (i, 0)),
                  pl.BlockSpec((k, tn), lambda i, j: (0, j))],
        out_specs=pl.BlockSpec((tm, tn), lambda i, j: (i, j)),
        out_shape=jax.ShapeDtypeStruct((m, n), F32),
        compiler_params=_cparams("parallel", "parallel"),
        name="mm",
    )(a, w)


def _mm_res_kernel(*refs, n_in):
    a_refs, w_refs = refs[:n_in], refs[n_in:2 * n_in]
    x_ref, g_ref, o_ref = refs[2 * n_in:]
    acc = jnp.dot(a_refs[0][...], w_refs[0][...], preferred_element_type=F32)
    for a_ref, w_ref in zip(a_refs[1:], w_refs[1:]):
        acc += jnp.dot(a_ref[...], w_ref[...], preferred_element_type=F32)
    o_ref[...] = x_ref[...] + g_ref[0] * acc


def _mm_res(lay, a_list, w, x, mod, gate_idx, tm, tn):
    m, n = x.shape
    bpt = tm // BLK
    cond = lambda i: lay.cond_of_block(i * bpt)
    ka = a_list[0].shape[1]
    assert all(a.shape[1] == ka for a in a_list) and w.shape[0] == ka * len(a_list)
    in_specs = ([pl.BlockSpec((tm, ka), lambda i, j: (i, 0)) for _ in a_list]
                + [pl.BlockSpec((ka, tn), functools.partial(lambda i, j, r: (r, j), r=r))
                   for r in range(len(a_list))]
                + [pl.BlockSpec((tm, tn), lambda i, j: (i, j)),
                   pl.BlockSpec((1, 1, tn), lambda i, j: (cond(i), 0, gate_idx * (n // tn) + j))])
    return pl.pallas_call(
        functools.partial(_mm_res_kernel, n_in=len(a_list)),
        grid=(m // tm, n // tn),
        in_specs=in_specs,
        out_specs=pl.BlockSpec((tm, tn), lambda i, j: (i, j)),
        out_shape=jax.ShapeDtypeStruct((m, n), F32),
        compiler_params=_cparams("parallel", "parallel"),
        name="mm_res",
    )(*a_list, *([w] * len(a_list)), x, mod)


def _mm_glu_kernel(a_ref, wa_ref, wb_ref, o_ref):
    a = a_ref[...]
    u = jnp.dot(a, wa_ref[...], preferred_element_type=F32)
    v = jnp.dot(a, wb_ref[...], preferred_element_type=F32)
    o_ref[...] = u * jax.nn.sigmoid(v)


def _mm_glu(a, w, tm, tn):
    m, k = a.shape
    n = w.shape[1] // 2
    nj = n // tn
    return pl.pallas_call(
        _mm_glu_kernel,
        grid=(m // tm, nj),
        in_specs=[pl.BlockSpec((tm, k), lambda i, j: (i, 0)),
                  pl.BlockSpec((k, tn), lambda i, j: (0, j)),
                  pl.BlockSpec((k, tn), lambda i, j: (0, j + nj))],
        out_specs=pl.BlockSpec((tm, tn), lambda i, j: (i, j)),
        out_shape=jax.ShapeDtypeStruct((m, n), F32),
        compiler_params=_cparams("parallel", "parallel"),
        name="mm_glu",
    )(a, w, w)


def _row_index(shape):
    return lax.broadcasted_iota(jnp.int32, shape, 0)


def _ffn_up_kernel(a_ref, wg_ref, wv_ref, dg_ref, dv_ref, o_ref, *, lay, tm):
    i = pl.program_id(0)
    seq_len = jnp.where(i * (tm // BLK) < lay.pb, BLK, lay.bps * BLK)
    a = a_ref[...]

    def conv(w_ref, d_ref):
        u = jnp.dot(a, w_ref[...], preferred_element_type=F32)
        pos = _row_index(u.shape) & (seq_len - 1)
        prev = jnp.where(pos == 0, 0.0, pltpu.roll(u, 1, axis=0))
        nxt = jnp.where(pos == seq_len - 1, 0.0, pltpu.roll(u, tm - 1, axis=0))
        return prev * d_ref[0:1, :] + u * d_ref[1:2, :] + nxt * d_ref[2:3, :]

    gate = conv(wg_ref, dg_ref)
    val = conv(wv_ref, dv_ref)
    o_ref[...] = (gate * jax.nn.sigmoid(gate) * val).astype(o_ref.dtype)


def _ffn_up(lay, a, w_gate, w_val, dw_gate, dw_val, tm, tn):
    m, k = a.shape
    f = w_gate.shape[1]
    w_spec = pl.BlockSpec((k, tn), lambda i, j: (0, j))
    dw_spec = pl.BlockSpec((3, tn), lambda i, j: (0, j))
    return pl.pallas_call(
        functools.partial(_ffn_up_kernel, lay=lay, tm=tm),
        grid=(m // tm, pl.cdiv(f, tn)),
        in_specs=[pl.BlockSpec((tm, k), lambda i, j: (i, 0)), w_spec, w_spec, dw_spec, dw_spec],
        out_specs=pl.BlockSpec((tm, tn), lambda i, j: (i, j)),
        out_shape=jax.ShapeDtypeStruct((m, f), BF16),
        compiler_params=_cparams("parallel", "parallel"),
        name="ffn_up",
    )(a, w_gate, w_val, dw_gate, dw_val)


def _split_bf16(x):
    parts = []
    r = x
    for s in range(N_SPLIT):
        p = r.astype(BF16)
        parts.append(p)
        if s + 1 < N_SPLIT:
            r = r - p.astype(F32)
    return parts


def _seg_sum(x, e_ref):
    lhs = jnp.concatenate(_split_bf16(x), axis=1)
    return jnp.dot(lhs, e_ref[...], preferred_element_type=F32)


def _seg_ones():
    r = jnp.arange(N_SPLIT * LANES)[:, None] % LANES
    c = jnp.arange(LANES)[None, :]
    return (r // HEAD_B == c // HEAD_B).astype(BF16)


def _seg_ones_pair():
    r = jnp.arange(2 * LANES)[:, None]
    c = jnp.arange(2 * LANES)[None, :]
    return (r // HEAD_B == c // HEAD_B).astype(BF16)


def _softplus(z):
    return jnp.maximum(z, 0.0) + jnp.log(1.0 + jnp.exp(-jnp.abs(z)))


def _prep_kernel(p_ref, pp_ref, pn_ref, mu_ref, w0_ref, w2_ref, a0_ref, a2_ref, g2_ref,
                 kk_ref, ka_ref, rk_ref, e_ref,
                 r_o, v_o, a_o, wf_o, kf_o, bf_o, wb_o, kb_o, bb_o, g_o, bonus_o, *, lay):
    i = pl.program_id(0)
    first, last = lay.seq_pos(i)
    rows = _row_index((BLK, 1))

    def shifted(c0, c1):
        p = p_ref[:, c0:c1]
        prev_row = jnp.where(first, 0.0, pp_ref[SUBLANES - 1:SUBLANES, c0:c1])
        next_row = jnp.where(last, 0.0, pn_ref[0:1, c0:c1])
        p_prev = jnp.where(rows == 0, prev_row, pltpu.roll(p, 1, axis=0))
        p_next = jnp.where(rows == BLK - 1, next_row, pltpu.roll(p, BLK - 1, axis=0))
        return p + mu_ref[0:1, c0:c1] * (p_prev - p) + mu_ref[1:2, c0:c1] * (p_next - p)

    c_w = 3 * D_B
    c_a = c_w + LANES
    c_g = c_a + LANES
    tw = jnp.tanh(shifted(c_w, c_a)).astype(BF16)
    xa = shifted(c_a, c_g).astype(BF16)
    sg = jax.nn.sigmoid(shifted(c_g, D_REST)).astype(BF16)

    for j in range(N_PAIR):
        sl = slice(j * LANES, (j + 1) * LANES)
        r = shifted(j * LANES, (j + 1) * LANES)
        kb = shifted(D_B + j * LANES, D_B + (j + 1) * LANES)
        vb = shifted(2 * D_B + j * LANES, 2 * D_B + (j + 1) * LANES)
        r_o[:, sl] = r
        v_o[:, sl] = vb
        g_o[:, sl] = jnp.dot(sg, g2_ref[:, sl], preferred_element_type=F32)
        kk = kb * kk_ref[:, sl]
        nrm = jnp.sqrt(_seg_sum(kk * kk, e_ref))
        kk = kk / jnp.maximum(nrm, 1e-12)
        a_o[:, sl] = -kk
        bonus_o[:, sl] = _seg_sum(r * kb * rk_ref[:, sl], e_ref) * vb
        for d, (w_o, k_o, b_o) in enumerate(((wf_o, kf_o, bf_o), (wb_o, kb_o, bb_o))):
            lw = w0_ref[d:d + 1, sl] + jnp.dot(tw, w2_ref[d, :, sl], preferred_element_type=F32)
            w_log = -_softplus(-lw) - 0.5
            w_o[:, sl] = jnp.exp(-jnp.exp(w_log))
            a_lr = jax.nn.sigmoid(a0_ref[d:d + 1, sl]
                                  + jnp.dot(xa, a2_ref[d, :, sl], preferred_element_type=F32))
            k_o[:, sl] = kb * (1.0 + (a_lr - 1.0) * ka_ref[:, sl])
            b_o[:, sl] = kk * a_lr


def _rwkv_prep(lay, proj, mu, w0, w2p, a0, a2p, g2, k_k, k_a, r_k, e3):
    nb = lay.nb
    hb = BLK // SUBLANES
    n_halo = lay.rows // SUBLANES
    full = lambda shape: pl.BlockSpec(shape, lambda i: (0,) * len(shape))
    out = jax.ShapeDtypeStruct((lay.rows, D_B), F32)
    return pl.pallas_call(
        functools.partial(_prep_kernel, lay=lay),
        grid=(nb,),
        in_specs=[pl.BlockSpec((BLK, D_REST), lambda i: (i, 0)),
                  pl.BlockSpec((SUBLANES, D_REST), lambda i: (jnp.maximum(i * hb - 1, 0), 0)),
                  pl.BlockSpec((SUBLANES, D_REST), lambda i: (jnp.minimum((i + 1) * hb, n_halo - 1), 0)),
                  full((2, D_REST)), full((2, D_B)), full((2, LANES, D_B)), full((2, D_B)),
                  full((2, LANES, D_B)), full((LORA_G, D_B)), full((1, D_B)), full((1, D_B)),
                  full((1, D_B)), full((N_SPLIT * LANES, LANES))],
        out_specs=[pl.BlockSpec((BLK, D_B), lambda i: (i, 0))] * 11,
        out_shape=[out] * 11,
        compiler_params=_cparams("parallel"),
        name="rwkv_prep",
    )(proj, proj, proj, mu, w0, w2p, a0, a2p, g2, k_k, k_a, r_k, e3)


def _scan_kernel(rf, vf, af, wf, kf, bf, rb, vb, ab, wb, kb, bb, init_ref, e_ref,
                 yf_ref, yb_ref, fin_ref, st, ybuf, *, lay):
    i = pl.program_id(0)
    first, _ = lay.seq_pos(i)

    @pl.when(first)
    def _():
        st[...] = jnp.where(i >= lay.pb, init_ref[0], 0.0)

    diag = (lax.broadcasted_iota(jnp.int32, (HEAD_B, LANES), 0)
            == lax.broadcasted_iota(jnp.int32, (HEAD_B, LANES), 1) % HEAD_B)
    dirs = ((rf, vf, af, wf, kf, bf, yf_ref), (rb, vb, ab, wb, kb, bb, yb_ref))

    def seg_sums(tiles):
        bt = [t.astype(BF16) for t in tiles]
        lhs = jnp.concatenate([jnp.concatenate(bt[k:k + 2], axis=1) for k in range(0, len(bt), 2)], axis=0)
        res = jnp.dot(lhs, e_ref[...], preferred_element_type=F32)
        return [res[(k // 2) * HEAD_B:(k // 2 + 1) * HEAD_B, (k % 2) * LANES:(k % 2 + 1) * LANES]
                for k in range(len(bt))]

    def group(gi, carry):
        bases = (pl.multiple_of(gi * SUBLANES, SUBLANES),
                 pl.multiple_of(BLK - SUBLANES - gi * SUBLANES, SUBLANES))

        def row(ref, d, j, sl):
            return ref[pl.ds(bases[d], SUBLANES), sl][j:j + 1, :]

        def emit_y(js, tiles):
            for d in range(2):
                for p in range(N_PAIR):
                    ybuf[d, js[d]:js[d] + 1, p * LANES:(p + 1) * LANES] = jnp.sum(
                        jnp.where(diag, tiles[d * N_PAIR + p], 0.0), axis=0, keepdims=True)

        js_prev = None
        for t in range(SUBLANES):
            js = (t, SUBLANES - 1 - t)
            xs, ys = [], []
            for d in range(2):
                r_r, v_r, a_r = dirs[d][0], dirs[d][1], dirs[d][2]
                for p in range(N_PAIR):
                    sl = slice(p * LANES, (p + 1) * LANES)
                    s_cur = st[d, p]
                    xs.append(s_cur * row(a_r, d, js[d], sl))
                    xs.append(jnp.where(diag, row(v_r, d, js[d], sl), 0.0))
                    if js_prev is not None:
                        ys.append(s_cur * row(r_r, d, js_prev[d], sl))
            res = seg_sums(xs + ys)
            if js_prev is not None:
                emit_y(js_prev, res[len(xs):])
            for d in range(2):
                w_r, k_r, b_r = dirs[d][3], dirs[d][4], dirs[d][5]
                for p in range(N_PAIR):
                    sl = slice(p * LANES, (p + 1) * LANES)
                    idx = d * N_PAIR + p
                    sa, vcol = res[2 * idx], res[2 * idx + 1]
                    st[d, p] = (st[d, p] * row(w_r, d, js[d], sl) + sa * row(b_r, d, js[d], sl)
                                + vcol * row(k_r, d, js[d], sl))
            js_prev = js
        emit_y(js_prev, seg_sums([st[d, p] * row(dirs[d][0], d, js_prev[d], slice(p * LANES, (p + 1) * LANES))
                                  for d in range(2) for p in range(N_PAIR)]))
        for d in range(2):
            dirs[d][6][pl.ds(bases[d], SUBLANES), :] = ybuf[d]
        return carry

    lax.fori_loop(0, BLK // SUBLANES, group, 0)
    fin_ref[0] = st[...]


def _rwkv_scan(lay, r, v, a, wf, kf, bf, wb, kb, bb, init, e3):
    pb, bps = lay.pb, lay.bps

    def bwd_block(i):
        c = (i - pb) % bps
        return jnp.where(i < pb, i, i - c + (bps - 1 - c))

    fwd = pl.BlockSpec((BLK, D_B), lambda i: (i, 0))
    bwd = pl.BlockSpec((BLK, D_B), lambda i: (bwd_block(i), 0))
    st_shape = (2, N_PAIR, HEAD_B, LANES)
    y = jax.ShapeDtypeStruct((lay.rows, D_B), F32)
    return pl.pallas_call(
        functools.partial(_scan_kernel, lay=lay),
        grid=(lay.nb,),
        in_specs=[fwd] * 6 + [bwd] * 6
        + [pl.BlockSpec((1,) + st_shape, lambda i: (jnp.maximum(i - pb, 0) // bps, 0, 0, 0, 0)),
           pl.BlockSpec((2 * LANES, 2 * LANES), lambda i: (0, 0))],
        out_specs=[fwd, bwd, pl.BlockSpec((1,) + st_shape, lambda i: (i, 0, 0, 0, 0))],
        out_shape=[y, y, jax.ShapeDtypeStruct((lay.nb,) + st_shape, F32)],
        scratch_shapes=[pltpu.VMEM(st_shape, F32), pltpu.VMEM((2, SUBLANES, D_B), F32)],
        compiler_params=_cparams("arbitrary"),
        name="rwkv_scan",
    )(r, v, a, wf, kf, bf, r, v, a, wb, kb, bb, init, e3)


def _post_kernel(yf_ref, yb_ref, bonus_ref, g_ref, lg_ref, lb_ref, e_ref, o_ref):
    for j in range(N_PAIR):
        sl = slice(j * LANES, (j + 1) * LANES)
        y = yf_ref[:, sl] + yb_ref[:, sl]
        mu = _seg_sum(y, e_ref) * (1.0 / HEAD_B)
        dlt = y - mu
        var = _seg_sum(dlt * dlt, e_ref) * (1.0 / HEAD_B)
        yn = dlt * lax.rsqrt(var + GN_EPS) * lg_ref[:, sl] + lb_ref[:, sl]
        o_ref[:, sl] = ((yn + bonus_ref[:, sl]) * g_ref[:, sl]).astype(o_ref.dtype)


def _rwkv_post(lay, yf, yb, bonus, g, lnx_g, lnx_b, e3):
    blk = pl.BlockSpec((BLK, D_B), lambda i: (i, 0))
    vec = pl.BlockSpec((1, D_B), lambda i: (0, 0))
    return pl.pallas_call(
        _post_kernel,
        grid=(lay.nb,),
        in_specs=[blk, blk, blk, blk, vec, vec, pl.BlockSpec((N_SPLIT * LANES, LANES), lambda i: (0, 0))],
        out_specs=blk,
        out_shape=jax.ShapeDtypeStruct((lay.rows, D_B), BF16),
        compiler_params=_cparams("parallel"),
        name="rwkv_post",
    )(yf, yb, bonus, g, lnx_g, lnx_b, e3)


def _lambda_value(dl_ref):
    dl = dl_ref[...]
    s01 = jnp.sum(dl[0:1] * dl[1:2], axis=-1, keepdims=True)
    s23 = jnp.sum(dl[2:3] * dl[3:4], axis=-1, keepdims=True)
    return jnp.exp(s01) - jnp.exp(s23) + LAM_INIT0


def _softmax_rows(s):
    e = jnp.exp(s - jnp.max(s, axis=-1, keepdims=True))
    return e / jnp.sum(e, axis=-1, keepdims=True)


def _diff_attend(q, k_bf, v_bf, lam, g_row):
    lane = lax.broadcasted_iota(jnp.int32, q.shape, 1)
    nt = (((1,), (1,)), ((), ()))
    q1 = jnp.where(lane < HEAD_QK, q, 0.0).astype(BF16)
    q2 = jnp.where(lane >= HEAD_QK, q, 0.0).astype(BF16)
    p1 = _softmax_rows(lax.dot_general(q1, k_bf, nt, preferred_element_type=F32))
    p2 = _softmax_rows(lax.dot_general(q2, k_bf, nt, preferred_element_type=F32))
    a = (p1 - lam * p2).astype(BF16)
    o = jnp.dot(a, v_bf, preferred_element_type=F32)
    o = o * lax.rsqrt(jnp.mean(o * o, axis=-1, keepdims=True) + LN_EPS) * g_row
    return o * (1.0 - LAM_INIT0)


def _attn_prompt_kernel(q_ref, k_ref, v_ref, dl_ref, g_ref, o_ref, ck_ref, cv_ref):
    k = k_ref[...]
    v = v_ref[...]
    ck_ref[0, 0, 0, 0] = k[:, :HEAD_QK]
    ck_ref[0, 0, 0, 1] = k[:, HEAD_QK:]
    cv_ref[0, 0, 0] = v
    o = _diff_attend(q_ref[...], (k * HEAD_QK ** -0.5).astype(BF16), v.astype(BF16),
                     _lambda_value(dl_ref), g_ref[...])
    o_ref[...] = o.astype(o_ref.dtype)


def _attn_prompt(lay, proj, diff_lambda, subln_g):
    qb, kb, vb = D_REST // LANES, (D_REST + D_A) // LANES, (D_REST + 2 * D_A) // LANES
    n = lay.n_prompt
    return pl.pallas_call(
        _attn_prompt_kernel,
        grid=(n, H_A),
        in_specs=[pl.BlockSpec((BLK, LANES), lambda b, h: (b, qb + h)),
                  pl.BlockSpec((BLK, LANES), lambda b, h: (b, kb + h)),
                  pl.BlockSpec((BLK, LANES), lambda b, h: (b, vb + h)),
                  pl.BlockSpec((4, HEAD_QK), lambda b, h: (0, 0)),
                  pl.BlockSpec((1, HEAD_V), lambda b, h: (0, 0))],
        out_specs=[pl.BlockSpec((BLK, LANES), lambda b, h: (b, h)),
                   pl.BlockSpec((1, 1, 1, 2, BLK, HEAD_QK), lambda b, h: (b, 0, h, 0, 0, 0)),
                   pl.BlockSpec((1, 1, 1, BLK, HEAD_V), lambda b, h: (b, 0, h, 0, 0))],
        out_shape=[jax.ShapeDtypeStruct((n * BLK, D_A), BF16),
                   jax.ShapeDtypeStruct((n, 1, H_A, 2, BLK, HEAD_QK), F32),
                   jax.ShapeDtypeStruct((n, 1, H_A, BLK, HEAD_V), F32)],
        compiler_params=_cparams("parallel", "parallel"),
        name="attn_prompt",
    )(proj, proj, proj, diff_lambda, subln_g.reshape(1, HEAD_V))


def _rope(x, cos, sin_signed):
    lane = lax.broadcasted_iota(jnp.int32, x.shape, 1)
    partner = jnp.where(lane % 32 < 16, pltpu.roll(x, LANES - 16, axis=1), pltpu.roll(x, 16, axis=1))
    return x * cos + partner * sin_signed


def _attn_sample_kernel(q_ref, k_ref, v_ref, ck_ref, cv_ref, cq_ref, sq_ref, ck_t_ref, sk_t_ref,
                        dl_ref, g_ref, o_ref, k_all, v_all, *, past, seq):
    @pl.when(pl.program_id(2) == 0)
    def _():
        scale = HEAD_QK ** -0.5
        k_all[0:past, :] = (jnp.concatenate([ck_ref[0, 0, 0, 0], ck_ref[0, 0, 0, 1]], axis=1)
                            * scale).astype(BF16)
        k_all[past:past + seq, :] = (_rope(k_ref[...], ck_t_ref[...], sk_t_ref[...]) * scale).astype(BF16)
        v_all[0:past, :] = cv_ref[0, 0, 0].astype(BF16)
        v_all[past:past + seq, :] = v_ref[...].astype(BF16)

    q = _rope(q_ref[...], cq_ref[...], sq_ref[...])
    o = _diff_attend(q, k_all[...], v_all[...], _lambda_value(dl_ref), g_ref[...])
    o_ref[...] = o.astype(o_ref.dtype)


def _attn_sample(lay, proj, cache_k, cache_v, cos_t, sin_t, diff_lambda, subln_g):
    qb, kb, vb = D_REST // LANES, (D_REST + D_A) // LANES, (D_REST + 2 * D_A) // LANES
    n, bps, pb = lay.n_sample, lay.bps, lay.pb
    seq = bps * BLK
    past = cache_k.shape[4]
    s0 = pb // bps
    assert pb % bps == 0
    tq = ATTN_Q_ROWS if seq % ATTN_Q_ROWS == 0 else BLK
    nq = seq // tq
    q0 = pb * BLK // tq
    return pl.pallas_call(
        functools.partial(_attn_sample_kernel, past=past, seq=seq),
        grid=(n, H_A, nq),
        in_specs=[pl.BlockSpec((tq, LANES), lambda b, h, t: (q0 + b * nq + t, qb + h)),
                  pl.BlockSpec((seq, LANES), lambda b, h, t: (s0 + b, kb + h)),
                  pl.BlockSpec((seq, LANES), lambda b, h, t: (s0 + b, vb + h)),
                  pl.BlockSpec((1, 1, 1, 2, past, HEAD_QK), lambda b, h, t: (b, 0, h, 0, 0, 0)),
                  pl.BlockSpec((1, 1, 1, past, HEAD_V), lambda b, h, t: (b, 0, h, 0, 0)),
                  pl.BlockSpec((tq, LANES), lambda b, h, t: (t, 0)),
                  pl.BlockSpec((tq, LANES), lambda b, h, t: (t, 0)),
                  pl.BlockSpec((seq, LANES), lambda b, h, t: (0, 0)),
                  pl.BlockSpec((seq, LANES), lambda b, h, t: (0, 0)),
                  pl.BlockSpec((4, HEAD_QK), lambda b, h, t: (0, 0)),
                  pl.BlockSpec((1, HEAD_V), lambda b, h, t: (0, 0))],
        out_specs=pl.BlockSpec((tq, LANES), lambda b, h, t: (b * nq + t, h)),
        out_shape=jax.ShapeDtypeStruct((n * seq, D_A), BF16),
        scratch_shapes=[pltpu.VMEM((past + seq, LANES), BF16), pltpu.VMEM((past + seq, LANES), BF16)],
        compiler_params=_cparams("parallel", "parallel", "arbitrary"),
        name="attn_sample",
    )(proj, proj, proj, cache_k, cache_v, cos_t, sin_t, cos_t, sin_t, diff_lambda,
      subln_g.reshape(1, HEAD_V))


def _rope_tables(seq):
    t = jnp.arange(seq)
    lane = jnp.arange(LANES)
    n_freq = HEAD_QK // 4
    inv = ROPE_BASE ** (-(lane % n_freq).astype(F32) / n_freq)
    use_col = (lane % HEAD_QK) >= HEAD_QK // 2
    pos = jnp.where(use_col[None, :], (t % GRID_W)[:, None], (t // GRID_W)[:, None]).astype(F32)
    ang = pos * inv[None, :]
    sin_signed = jnp.where((lane % (2 * n_freq) < n_freq)[None, :], -jnp.sin(ang), jnp.sin(ang))
    return jnp.cos(ang), sin_signed


CONV_T = 8
CONV_UNROLL = 4
CONV_SUB = 16


def _conv_ln_kernel(u_ref, up_ref, un_ref, w_ref, b_ref, g_ref, beta_ref, o_ref, pad, *, lay):
    i = pl.program_id(0)
    first, last = lay.seq_pos(i)
    pad[0:CONV_HALO] = jnp.where(first, 0.0, up_ref[...])
    pad[CONV_HALO:CONV_HALO + BLK] = u_ref[...]
    pad[CONV_HALO + BLK:] = jnp.where(last, 0.0, un_ref[...])
    off = CONV_HALO - CONV_PAD
    n_ch = CONV_SUB * LANES

    def chunk(c, carry):
        t0 = c * CONV_T
        acc = jnp.zeros((CONV_T, CONV_SUB, LANES), F32) + b_ref[...]
        for j in range(CONV_W):
            acc = acc + pad[pl.ds(t0 + off + j, CONV_T)] * w_ref[j]
        tot = lambda x: jnp.sum(jnp.sum(x, axis=2, keepdims=True), axis=1, keepdims=True)
        xc = acc - tot(acc) * (1.0 / n_ch)
        inv = lax.rsqrt(tot(xc * xc) * (1.0 / n_ch) + LN_EPS)
        y = xc * inv * g_ref[...] + beta_ref[...]
        o_ref[pl.ds(t0, CONV_T)] = (y * jax.nn.sigmoid(y)).astype(o_ref.dtype)
        return carry

    lax.fori_loop(0, BLK // CONV_T, chunk, 0, unroll=CONV_UNROLL)


def _conv_ln(lay, u, w_dw, b_dw, cln_g, cln_b):
    rows, d = u.shape
    assert d == CONV_SUB * LANES
    hb = BLK // CONV_HALO
    n_halo = rows // CONV_HALO
    as3 = lambda x: x.reshape(x.shape[0], CONV_SUB, LANES)
    vec = pl.BlockSpec((1, CONV_SUB, LANES), lambda i: (0, 0, 0))
    out = pl.pallas_call(
        functools.partial(_conv_ln_kernel, lay=lay),
        grid=(lay.nb,),
        in_specs=[pl.BlockSpec((BLK, CONV_SUB, LANES), lambda i: (i, 0, 0)),
                  pl.BlockSpec((CONV_HALO, CONV_SUB, LANES), lambda i: (jnp.maximum(i * hb - 1, 0), 0, 0)),
                  pl.BlockSpec((CONV_HALO, CONV_SUB, LANES),
                               lambda i: (jnp.minimum((i + 1) * hb, n_halo - 1), 0, 0)),
                  pl.BlockSpec((CONV_W, CONV_SUB, LANES), lambda i: (0, 0, 0)), vec, vec, vec],
        out_specs=pl.BlockSpec((BLK, CONV_SUB, LANES), lambda i: (i, 0, 0)),
        out_shape=jax.ShapeDtypeStruct((rows, CONV_SUB, LANES), BF16),
        scratch_shapes=[pltpu.VMEM((BLK + 2 * CONV_HALO, CONV_SUB, LANES), F32)],
        compiler_params=_cparams("parallel"),
        name="conv_ln",
    )(as3(u), as3(u), as3(u), as3(w_dw), as3(b_dw[None]), as3(cln_g[None]), as3(cln_b[None]))
    return out.reshape(rows, d)


def _pad_cols(x, n):
    return jnp.pad(x, ((0, 0), (0, n - x.shape[1])))


def _rest_layout(x):
    c = 3 * D_B
    return jnp.concatenate([x[:, :c], _pad_cols(x[:, c:c + LORA_W], LANES),
                            _pad_cols(x[:, c + LORA_W:c + LORA_W + LORA_A], LANES),
                            x[:, c + LORA_W + LORA_A:]], axis=1)


def _conv_ffn(lay, x, g, mod, w_up, w_dw, w_down, tm):
    h = _norm_mod(lay, x, g, mod, 3)
    u = _ffn_up(lay, h, w_up[:, :D_FF].astype(BF16), w_up[:, D_FF:].astype(BF16),
                w_dw[:, :D_FF], w_dw[:, D_FF:], tm, 512)
    return _mm_res(lay, [u], w_down.astype(BF16), x, mod, 5, tm, 512)


def _pair_state(s):
    b = s.shape[0]
    return s.reshape(b, N_PAIR, 2, HEAD_B, HEAD_B).transpose(0, 1, 3, 2, 4).reshape(b, N_PAIR, HEAD_B, LANES)


def _unpair_state(s):
    b = s.shape[0]
    return s.reshape(b, N_PAIR, HEAD_B, 2, HEAD_B).transpose(0, 1, 3, 2, 4).reshape(b, H_B, HEAD_B, HEAD_B)


def kernel(x_prompt, x_sample, cache_k, cache_v, state_wkv_fwd, state_wkv_bwd, c, c_ctx, w_ada, b_ada, norm_g, final_norm_g, w_in, w_out, diff_lambda, subln_g, shift_mu, w0, w2, a0, a2, g2, k_k, k_a, r_k, lnx_g, lnx_b, w_pw1, w_dw, b_dw, cln_g, cln_b, w_pw2, w_up, w_ffn_dw, w_down):
    n_prompt, seq_p, d = x_prompt.shape
    n_sample, seq_s, _ = x_sample.shape
    assert seq_p == BLK and seq_s % BLK == 0 and d == D_MODEL
    lay = _Layout(n_prompt, n_sample, seq_s)
    tm = lay.tile_rows()

    x = jnp.concatenate([x_prompt.reshape(-1, d), x_sample.reshape(-1, d)], axis=0)
    c16 = jnp.concatenate([c_ctx[None, :], c, jnp.zeros((N_COND - 1 - n_sample, d), F32)], axis=0)
    mod = _adaln(c16, w_ada, b_ada)
    mod = mod.reshape(mod.shape[0], N_COND, 1, 6 * d)
    e3 = _seg_ones()

    e = 0
    w_in_b = w_in[e].astype(BF16)
    w_in_p = jnp.concatenate([_rest_layout(w_in_b[:, 3 * D_A:]), w_in_b[:, :3 * D_A]], axis=1)
    mu_p = _rest_layout(shift_mu[e])
    pad_rows = lambda w: jnp.pad(w, ((0, 0), (0, LANES - w.shape[1]), (0, 0))).astype(BF16)

    h = _norm_mod(lay, x, norm_g[0, 0], mod[0], 0)
    proj = _mm(h, w_in_p, tm, 512)
    oa_p, new_k, new_v = _attn_prompt(lay, proj, diff_lambda[e], subln_g[e])
    cos_t, sin_t = _rope_tables(seq_s)
    oa_s = _attn_sample(lay, proj, cache_k[:, e:e + 1], cache_v[:, e:e + 1], cos_t, sin_t,
                        diff_lambda[e], subln_g[e])
    o_a = jnp.concatenate([oa_p, oa_s], axis=0)

    r_, v_, a_, wf, kf, bf, wb, kb, bb, g_, bonus = _rwkv_prep(
        lay, proj, mu_p, w0[e], pad_rows(w2[e]), a0[e], pad_rows(a2[e]), g2[e].astype(BF16),
        k_k[e].reshape(1, D_B), k_a[e].reshape(1, D_B), r_k[e].reshape(1, D_B), e3)
    init = jnp.stack([_pair_state(state_wkv_fwd[:, e]), _pair_state(state_wkv_bwd[:, e])], axis=1)
    yf, yb, fin = _rwkv_scan(lay, r_, v_, a_, wf, kf, bf, wb, kb, bb, init, _seg_ones_pair())
    o_b = _rwkv_post(lay, yf, yb, bonus, g_, lnx_g[e].reshape(1, D_B), lnx_b[e].reshape(1, D_B), e3)
    new_sf = _unpair_state(fin[:n_prompt, 0])[:, None]
    new_sb = _unpair_state(fin[:n_prompt, 1])[:, None]

    x = _mm_res(lay, [o_a, o_b], w_out[e].astype(BF16), x, mod[0], 2, tm, 512)
    x = _conv_ffn(lay, x, norm_g[0, 1], mod[0], w_up[0], w_ffn_dw[0], w_down[0], tm)

    o = 0
    h = _norm_mod(lay, x, norm_g[1, 0], mod[1], 0)
    u = _mm_glu(h, w_pw1[o].astype(BF16), tm, 512)
    u = _conv_ln(lay, u, w_dw[o], b_dw[o], cln_g[o], cln_b[o])
    x = _mm_res(lay, [u], w_pw2[o].astype(BF16), x, mod[1], 2, tm, 512)
    x = _conv_ffn(lay, x, norm_g[1, 1], mod[1], w_up[1], w_ffn_dw[1], w_down[1], tm)

    y_p = _final_norm(x, final_norm_g, 0, lay.pb).reshape(n_prompt, seq_p, d)
    y_s = _final_norm(x, final_norm_g, lay.pb, lay.nb - lay.pb).reshape(n_sample, seq_s, d)
    return (y_p, y_s, new_k, new_v, new_sf, new_sb)
```
